```python
import jax, jax.numpy as jnp
from jax import lax
import numpy as np

D_MODEL = 2048
BATCH = 8
SEQ = 2048
DEPTH = 2

CHUNK = 64
N_PREV_CHUNKS = 8
BAND = (N_PREV_CHUNKS + 1) * CHUNK

D_ATT = D_MODEL // 2
N_HEADS_A = 16
HEAD_DIM_A = D_ATT // N_HEADS_A
MAX_REL = 256
N_REL = CHUNK + MAX_REL

D_CONV = D_MODEL // 2
CONV_WIDTH = 3

D_SGU = D_MODEL // 2
SGU_CHUNK = 128
N_GROUPS_C = 8
GROUP_CH = D_SGU // N_GROUPS_C

N_BRANCH = 3
EPS = 1e-6
NEG_INF = -1e30

SPLIT_SIZES = (D_ATT, D_ATT, D_ATT, D_ATT,
               D_CONV, D_CONV, D_CONV, D_CONV,
               D_SGU, D_SGU, D_SGU,
               N_BRANCH * D_MODEL)
IN_COLS = sum(SPLIT_SIZES)

kernel_name = "hybrid_chunk_attn_conv_gmlp_sandwich"


def _rms_norm(x, g):
    xf = x.astype(jnp.float32)
    r = lax.rsqrt(jnp.mean(xf * xf, axis=-1, keepdims=True) + EPS)
    return (xf * r * g.astype(jnp.float32)).astype(x.dtype)


def _layer_norm(x, g, b):
    xf = x.astype(jnp.float32)
    mu = jnp.mean(xf, axis=-1, keepdims=True)
    var = jnp.mean(jnp.square(xf - mu), axis=-1, keepdims=True)
    y = (xf - mu) * lax.rsqrt(var + EPS) * g.astype(jnp.float32) + b.astype(jnp.float32)
    return y.astype(x.dtype)


def _band_bias(rel_bias):
    q_idx = jnp.arange(CHUNK)[:, None]
    k_idx = jnp.arange(BAND)[None, :]
    dist = q_idx + N_PREV_CHUNKS * CHUNK - k_idx
    idx = jnp.clip(dist, -(CHUNK - 1), MAX_REL) + (CHUNK - 1)
    return rel_bias[:, idx].astype(jnp.float32)


def _chunk_band_attention(q, k, v, rel_bias):
    b, s, h, dh = q.shape
    n_chunks = s // CHUNK
    pad = N_PREV_CHUNKS * CHUNK
    kp = jnp.pad(k, ((0, 0), (pad, 0), (0, 0), (0, 0)))
    vp = jnp.pad(v, ((0, 0), (pad, 0), (0, 0), (0, 0)))
    qc = q.reshape(b, n_chunks, CHUNK, h, dh).transpose(1, 0, 2, 3, 4)
    bias = _band_bias(rel_bias)
    scale = HEAD_DIM_A ** -0.5
    k_off = jnp.arange(BAND)

    def one_chunk(args):
        c, qb = args
        start = c * CHUNK
        kb = lax.dynamic_slice_in_dim(kp, start, BAND, axis=1)
        vb = lax.dynamic_slice_in_dim(vp, start, BAND, axis=1)
        sc = jnp.einsum('bqhd,bkhd->bhqk', qb, kb).astype(jnp.float32) * scale + bias
        valid = (start - pad + k_off) >= 0
        sc = jnp.where(valid[None, None, None, :], sc, NEG_INF)
        p = jax.nn.softmax(sc, axis=-1).astype(vb.dtype)
        return jnp.einsum('bhqk,bkhd->bqhd', p, vb)

    out = lax.map(one_chunk, (jnp.arange(n_chunks), qc))
    return out.transpose(1, 0, 2, 3, 4).reshape(b, s, h * dh)


def _causal_dwconv(x, w):
    return lax.conv_general_dilated(
        x, w[:, None, :].astype(x.dtype), window_strides=(1,),
        padding=[(CONV_WIDTH - 1, 0)], dimension_numbers=('NWC', 'WIO', 'NWC'),
        feature_group_count=x.shape[-1])


def _spatial_gating(u, v, ln_g, ln_b, sp_w, sp_b):
    b, s, _ = v.shape
    vn = _layer_norm(v, ln_g, ln_b)
    vg = vn.reshape(b, s // SGU_CHUNK, SGU_CHUNK, N_GROUPS_C, GROUP_CH)
    mask = jnp.tril(jnp.ones((SGU_CHUNK, SGU_CHUNK), dtype=sp_w.dtype))
    mixed = jnp.einsum('gts,bnsgc->bntgc', sp_w * mask, vg)
    mixed = mixed + sp_b.T[None, None, :, :, None]
    return u * mixed.reshape(b, s, D_SGU)


def _layer(x, pre_g, w_in, rel_bias, conv_w, ln_g, ln_b, sp_w, sp_b,
           w_a, w_b, w_c, w_out, post_g):
    b, s, _ = x.shape
    h = _rms_norm(x, pre_g)
    z = h @ w_in
    points = [int(p) for p in np.cumsum(SPLIT_SIZES)[:-1]]
    (q, k, v, g_a, gate_bb, gate_cc, h_b, g_b,
     u, v_c, g_c, m) = jnp.split(z, points, axis=-1)

    heads = lambda t: t.reshape(b, s, N_HEADS_A, HEAD_DIM_A)
    o_a = _chunk_band_attention(heads(q), heads(k), heads(v), rel_bias)
    y_a = (o_a * jax.nn.silu(g_a)) @ w_a

    o_b = gate_bb * _causal_dwconv(gate_cc * h_b, conv_w)
    y_b = (o_b * jax.nn.silu(g_b)) @ w_b

    o_c = _spatial_gating(jax.nn.gelu(u), jax.nn.gelu(v_c), ln_g, ln_b, sp_w, sp_b)
    y_c = (o_c * jax.nn.silu(g_c)) @ w_c

    gts = jax.nn.sigmoid(m.astype(jnp.float32)).astype(x.dtype)
    gt_a, gt_b, gt_c = jnp.split(gts, N_BRANCH, axis=-1)
    merged = gt_a * y_a + gt_b * y_b + gt_c * y_c
    y = merged @ w_out
    return x + _rms_norm(y, post_g)


def setup_inputs(seed: int = 0) -> dict:
    key = jax.random.key(seed)
    ks = jax.random.split(key, 16)
    L = DEPTH
    nrm = lambda k, shape, sc: jax.random.normal(k, shape, jnp.float32) * sc
    return {
        "x": nrm(ks[0], (BATCH, SEQ, D_MODEL), 1.0),
        "pre_norm": 1.0 + nrm(ks[1], (L, D_MODEL), 0.02),
        "w_in": nrm(ks[2], (L, D_MODEL, IN_COLS), D_MODEL ** -0.5),
        "rel_bias": nrm(ks[3], (L, N_HEADS_A, N_REL), 0.2),
        "conv_w": nrm(ks[4], (L, CONV_WIDTH, D_CONV), 0.5),
        "sgu_ln_g": 1.0 + nrm(ks[5], (L, D_SGU), 0.02),
        "sgu_ln_b": nrm(ks[6], (L, D_SGU), 0.02),
        "spatial_w": nrm(ks[7], (L, N_GROUPS_C, SGU_CHUNK, SGU_CHUNK), SGU_CHUNK ** -0.5),
        "spatial_b": 1.0 + nrm(ks[8], (L, N_GROUPS_C, SGU_CHUNK), 0.1),
        "w_branch_a": nrm(ks[9], (L, D_ATT, D_MODEL), D_ATT ** -0.5),
        "w_branch_b": nrm(ks[10], (L, D_CONV, D_MODEL), D_CONV ** -0.5),
        "w_branch_c": nrm(ks[11], (L, D_SGU, D_MODEL), D_SGU ** -0.5),
        "w_out": nrm(ks[12], (L, D_MODEL, D_MODEL), D_MODEL ** -0.5),
        "post_norm": 1.0 + nrm(ks[13], (L, D_MODEL), 0.02),
    }


def reference(x, pre_norm, w_in, rel_bias, conv_w, sgu_ln_g, sgu_ln_b,
              spatial_w, spatial_b, w_branch_a, w_branch_b, w_branch_c,
              w_out, post_norm):
    for i in range(DEPTH):
        x = _layer(x, pre_norm[i], w_in[i], rel_bias[i], conv_w[i],
                   sgu_ln_g[i], sgu_ln_b[i], spatial_w[i], spatial_b[i],
                   w_branch_a[i], w_branch_b[i], w_branch_c[i],
                   w_out[i], post_norm[i])
    return x
```

```python
import functools
import math

import jax
import jax.numpy as jnp
import numpy as np
from jax import lax
from jax.experimental import pallas as pl
from jax.experimental.pallas import tpu as pltpu

F32 = jnp.float32
BF16 = jnp.bfloat16

D_MODEL = 2048
CHUNK = 64
N_PREV = 8
BAND = (N_PREV + 1) * CHUNK
D_BR = D_MODEL // 2
N_HEADS = 16
HEAD_DIM = D_BR // N_HEADS
MAX_REL = 256
CONV_WIDTH = 3
SGU_CHUNK = 128
N_GROUPS = 8
GROUP_CH = D_BR // N_GROUPS
EPS = 1e-6
NEG_INF = -1e30

N_COL_BLOCKS = 11 + 3 * (D_MODEL // D_BR)
IN_COLS = N_COL_BLOCKS * D_BR

COL_Q, COL_K, COL_V, COL_GA, COL_B, COL_C, COL_HB, COL_GB, COL_U, COL_VC, COL_GC, COL_M = range(12)

LANES = 128
HEAD_PAIRS = D_BR // LANES
HALO_ROWS = 16

TM_IN = 1024
TM_OUT = 256
VMEM_LIMIT = 56 * 1024 * 1024


def _act_table():
    s = math.sqrt(2.0 / math.pi)
    ident = (1.0, 0.0, 0.0, 0.0, 0.0, 0.0)
    silu = (0.5, 0.0, 0.5, 0.0, 0.5, 0.0)
    gelu = (0.5, 0.0, 0.5, 0.0, s, s * 0.044715)
    sigm = (0.0, 0.5, 0.0, 0.5, 0.5, 0.0)
    rows = [ident] * N_COL_BLOCKS
    rows[COL_Q] = (HEAD_DIM ** -0.5, 0.0, 0.0, 0.0, 0.0, 0.0)
    for c in (COL_GA, COL_GB, COL_GC):
        rows[c] = silu
    for c in (COL_U, COL_VC):
        rows[c] = gelu
    for c in range(COL_M, N_COL_BLOCKS):
        rows[c] = sigm
    return np.asarray(rows, np.float32)


def _in_proj_kernel(coef_ref, x_ref, g_ref, w_ref, z_ref, h_ref):
    j = pl.program_id(1)

    @pl.when(j == 0)
    def _():
        x = x_ref[...]
        r = lax.rsqrt(jnp.mean(x * x, axis=-1, keepdims=True) + EPS)
        h_ref[...] = (x * r * g_ref[...]).astype(BF16)

    z = jnp.dot(h_ref[...], w_ref[...], preferred_element_type=F32)
    a, b, c, d, k, l = (coef_ref[j, t] for t in range(6))
    t = jnp.tanh(z * (k + l * (z * z)))
    z_ref[...] = (a * z + b + (c * z + d) * t).astype(BF16)


def _in_proj(x2, pre_g, w_in_bf16):
    n = x2.shape[0]
    coef = jnp.asarray(_act_table())
    return pl.pallas_call(
        _in_proj_kernel,
        grid=(n // TM_IN, N_COL_BLOCKS),
        in_specs=[
            pl.BlockSpec(memory_space=pltpu.SMEM),
            pl.BlockSpec((TM_IN, D_MODEL), lambda i, j: (i, 0)),
            pl.BlockSpec((1, D_MODEL), lambda i, j: (0, 0)),
            pl.BlockSpec((D_MODEL, D_BR), lambda i, j: (0, j)),
        ],
        out_specs=pl.BlockSpec((TM_IN, D_BR), lambda i, j: (i, j)),
        out_shape=jax.ShapeDtypeStruct((n, IN_COLS), BF16),
        scratch_shapes=[pltpu.VMEM((TM_IN, D_MODEL), BF16)],
        compiler_params=pltpu.CompilerParams(
            dimension_semantics=("arbitrary", "arbitrary"),
            vmem_limit_bytes=VMEM_LIMIT),
        name="in_proj",
    )(coef, x2, pre_g.reshape(1, D_MODEL), w_in_bf16)


def _attn_kernel(q_ref, k_ref, v_ref, g_ref, bias_ref, o_ref):
    c = pl.program_id(1)
    start = pl.multiple_of(jnp.maximum(c - N_PREV, 0) * CHUNK, CHUNK)
    low_head = lax.broadcasted_iota(jnp.int32, (CHUNK, LANES), 1) < HEAD_DIM
    for p in range(HEAD_PAIRS):
        cols = slice(p * LANES, (p + 1) * LANES)
        q = q_ref[0, :, cols]
        zero = jnp.zeros_like(q)
        q2 = jnp.concatenate([jnp.where(low_head, q, zero), jnp.where(low_head, zero, q)], axis=0)
        kw = k_ref[0, pl.ds(start, BAND), cols]
        s = lax.dot_general(q2, kw, (((1,), (1,)), ((), ())), preferred_element_type=F32)
        s = s + bias_ref[0, p]
        e = jnp.exp(s - jnp.max(s, axis=-1, keepdims=True))
        denom = jnp.sum(e, axis=-1, keepdims=True)
        vw = v_ref[0, pl.ds(start, BAND), cols]
        r = jnp.dot(e.astype(BF16), vw, preferred_element_type=F32) / denom
        o = jnp.where(low_head, r[:CHUNK], r[CHUNK:])
        o_ref[0, :, cols] = (o * g_ref[0, :, cols].astype(F32)).astype(BF16)


def _band_bias_table(rel_bias):
    width = BAND + N_PREV * CHUNK
    q_idx = jnp.arange(CHUNK)[:, None]
    k_idx = jnp.arange(width)[None, :]
    dist = q_idx + N_PREV * CHUNK - k_idx
    idx = jnp.clip(dist, -(CHUNK - 1), MAX_REL) + (CHUNK - 1)
    full = jnp.where(k_idx < BAND, rel_bias[:, idx].astype(F32), NEG_INF)
    slabs = jnp.stack([full[:, :, v * CHUNK: v * CHUNK + BAND] for v in range(N_PREV + 1)])
    return slabs.reshape(N_PREV + 1, HEAD_PAIRS, 2 * CHUNK, BAND)


def _attention(z3, bias_tab):
    b, s, _ = z3.shape
    return pl.pallas_call(
        _attn_kernel,
        grid=(b, s // CHUNK),
        in_specs=[
            pl.BlockSpec((1, CHUNK, D_BR), lambda i, c: (i, c, COL_Q)),
            pl.BlockSpec((1, s, D_BR), lambda i, c: (i, 0, COL_K)),
            pl.BlockSpec((1, s, D_BR), lambda i, c: (i, 0, COL_V)),
            pl.BlockSpec((1, CHUNK, D_BR), lambda i, c: (i, c, COL_GA)),
            pl.BlockSpec((1, HEAD_PAIRS, 2 * CHUNK, BAND),
                         lambda i, c: (jnp.maximum(N_PREV - c, 0), 0, 0, 0)),
        ],
        out_specs=pl.BlockSpec((1, CHUNK, D_BR), lambda i, c: (i, c, 0)),
        out_shape=jax.ShapeDtypeStruct((b, s, D_BR), BF16),
        compiler_params=pltpu.CompilerParams(
            dimension_semantics=("arbitrary", "arbitrary"),
            vmem_limit_bytes=VMEM_LIMIT),
        name="attn",
    )(z3, z3, z3, z3, bias_tab)


def _mix_out_kernel(tiles_per_seq,
                    a_ref, bg_ref, cg_ref, hb_ref, cprev_ref, hprev_ref, sgb_ref,
                    u_ref, vc_ref, sgc_ref,
                    ga0_ref, ga1_ref, gb0_ref, gb1_ref, gc0_ref, gc1_ref,
                    x_ref, convw_ref, lng_ref, lnb_ref, spw_ref, spb_ref,
                    wa_ref, wb_ref, wc_ref, wout_ref, postg_ref, o_ref):
    i = pl.program_id(0)
    tm = x_ref.shape[0]
    row = lax.broadcasted_iota(jnp.int32, (tm, D_BR), 0)

    ch = cg_ref[...].astype(F32) * hb_ref[...].astype(F32)
    prev = cprev_ref[...].astype(F32) * hprev_ref[...].astype(F32)
    prev = jnp.where(i % tiles_per_seq == 0, 0.0, prev)
    ch1 = jnp.where(row == 0, prev[HALO_ROWS - 1:HALO_ROWS], pltpu.roll(ch, 1, 0))
    ch2 = jnp.where(row == 0, prev[HALO_ROWS - 2:HALO_ROWS - 1],
                    jnp.where(row == 1, prev[HALO_ROWS - 1:HALO_ROWS], pltpu.roll(ch, 2, 0)))
    conv = convw_ref[0:1, :] * ch2 + convw_ref[1:2, :] * ch1 + convw_ref[2:3, :] * ch
    ob = (bg_ref[...].astype(F32) * conv * sgb_ref[...].astype(F32)).astype(BF16)

    v = vc_ref[...].astype(F32)
    mu = jnp.mean(v, axis=-1, keepdims=True)
    vc = v - mu
    var = jnp.mean(vc * vc, axis=-1, keepdims=True)
    vn = (vc * lax.rsqrt(var + EPS) * lng_ref[...] + lnb_ref[...]).astype(BF16)
    tri = (lax.broadcasted_iota(jnp.int32, (SGU_CHUNK, SGU_CHUNK), 0)
           >= lax.broadcasted_iota(jnp.int32, (SGU_CHUNK, SGU_CHUNK), 1))
    oc_rows = []
    for n in range(tm // SGU_CHUNK):
        rows = slice(n * SGU_CHUNK, (n + 1) * SGU_CHUNK)
        oc_cols = []
        for g in range(N_GROUPS):
            cols = slice(g * GROUP_CH, (g + 1) * GROUP_CH)
            w = jnp.where(tri, spw_ref[g], 0.0).astype(BF16)
            mixed = jnp.dot(w, vn[rows, cols], preferred_element_type=F32) + spb_ref[g]
            oc_cols.append(mixed)
        oc_rows.append(jnp.concatenate(oc_cols, axis=1))
    mixed = jnp.concatenate(oc_rows, axis=0)
    oc = (u_ref[...].astype(F32) * mixed * sgc_ref[...].astype(F32)).astype(BF16)

    ya = jnp.dot(a_ref[...], wa_ref[...], preferred_element_type=F32)
    yb = jnp.dot(ob, wb_ref[...], preferred_element_type=F32)
    yc = jnp.dot(oc, wc_ref[...], preferred_element_type=F32)
    ga = jnp.concatenate([ga0_ref[...], ga1_ref[...]], axis=1).astype(F32)
    gb = jnp.concatenate([gb0_ref[...], gb1_ref[...]], axis=1).astype(F32)
    gc = jnp.concatenate([gc0_ref[...], gc1_ref[...]], axis=1).astype(F32)
    merged = (ga * ya + gb * yb + gc * yc).astype(BF16)
    y = jnp.dot(merged, wout_ref[...], preferred_element_type=F32)
    r = lax.rsqrt(jnp.mean(y * y, axis=-1, keepdims=True) + EPS)
    o_ref[...] = x_ref[...] + y * r * postg_ref[...]


def _mix_out(a2, z2, x2, seq, conv_w, ln_g, ln_b, sp_w, sp_b, wa, wb, wc, wout, post_g):
    n = x2.shape[0]
    tm = TM_OUT
    halo_per_tile = tm // HALO_ROWS
    zblk = lambda col: pl.BlockSpec((tm, D_BR), lambda i, col=col: (i, col))
    halo = lambda col: pl.BlockSpec(
        (HALO_ROWS, D_BR), lambda i, col=col: (jnp.maximum(i * halo_per_tile - 1, 0), col))
    const = lambda shape: pl.BlockSpec(shape, lambda i: (0,) * len(shape),
                                       pipeline_mode=pl.Buffered(1))
    in_specs = [
        pl.BlockSpec((tm, D_BR), lambda i: (i, 0)),
        zblk(COL_B), zblk(COL_C), zblk(COL_HB), halo(COL_C), halo(COL_HB), zblk(COL_GB),
        zblk(COL_U), zblk(COL_VC), zblk(COL_GC),
        zblk(COL_M), zblk(COL_M + 1), zblk(COL_M + 2), zblk(COL_M + 3),
        zblk(COL_M + 4), zblk(COL_M + 5),
        pl.BlockSpec((tm, D_MODEL), lambda i: (i, 0)),
        const((CONV_WIDTH, D_BR)), const((1, D_BR)), const((1, D_BR)),
        const((N_GROUPS, SGU_CHUNK, SGU_CHUNK)), const((N_GROUPS, SGU_CHUNK, 1)),
        const((D_BR, D_MODEL)), const((D_BR, D_MODEL)), const((D_BR, D_MODEL)),
        const((D_MODEL, D_MODEL)), const((1, D_MODEL)),
    ]
    return pl.pallas_call(
        functools.partial(_mix_out_kernel, seq // tm),
        grid=(n // tm,),
        in_specs=in_specs,
        out_specs=pl.BlockSpec((tm, D_MODEL), lambda i: (i, 0)),
        out_shape=jax.ShapeDtypeStruct((n, D_MODEL), F32),
        compiler_params=pltpu.CompilerParams(
            dimension_semantics=("arbitrary",),
            vmem_limit_bytes=VMEM_LIMIT),
        name="mix_out",
    )(a2, *([z2] * 15), x2, conv_w, ln_g.reshape(1, D_BR), ln_b.reshape(1, D_BR),
      sp_w, sp_b.reshape(N_GROUPS, SGU_CHUNK, 1), wa, wb, wc, wout, post_g.reshape(1, D_MODEL))


def kernel(x, pre_norm, w_in, rel_bias, conv_w, sgu_ln_g, sgu_ln_b, spatial_w, spatial_b,
           w_branch_a, w_branch_b, w_branch_c, w_out, post_norm):
    b, s, d = x.shape
    n = b * s
    x2 = x.reshape(n, d)
    for i in range(pre_norm.shape[0]):
        z2 = _in_proj(x2, pre_norm[i], w_in[i].astype(BF16))
        a = _attention(z2.reshape(b, s, IN_COLS), _band_bias_table(rel_bias[i]))
        x2 = _mix_out(a.reshape(n, D_BR), z2, x2, s, conv_w[i], sgu_ln_g[i], sgu_ln_b[i],
                      spatial_w[i], spatial_b[i],
                      w_branch_a[i].astype(BF16), w_branch_b[i].astype(BF16),
                      w_branch_c[i].astype(BF16), w_out[i].astype(BF16), post_norm[i])
    return x2.reshape(b, s, d)
```

```python
import functools
import math

import jax
import jax.numpy as jnp
import numpy as np
from jax import lax
from jax.experimental import pallas as pl
from jax.experimental.pallas import tpu as pltpu

F32 = jnp.float32
BF16 = jnp.bfloat16

D_MODEL = 2048
CHUNK = 64
N_PREV = 8
D_BR = D_MODEL // 2
N_HEADS = 16
HEAD_DIM = D_BR // N_HEADS
MAX_REL = 256
CONV_WIDTH = 3
SGU_CHUNK = 128
N_GROUPS = 8
GROUP_CH = D_BR // N_GROUPS
EPS = 1e-6
NEG_INF = -1e30

N_COL_BLOCKS = 11 + 3 * (D_MODEL // D_BR)
IN_COLS = N_COL_BLOCKS * D_BR

COL_Q, COL_K, COL_V, COL_GA, COL_B, COL_C, COL_HB, COL_GB, COL_U, COL_VC, COL_GC, COL_M = range(12)

LANES = 128
HEAD_PAIRS = D_BR // LANES
HALO_ROWS = 16

TM_NORM = 512
TM_IN = 2048
SUB_M, SUB_N = 1024, 1024
TM_OUT = 256
VMEM_LIMIT = 56 * 1024 * 1024

Q_TILE = 2 * CHUNK
WIN = (N_PREV + 2) * CHUNK
BIAS_W = WIN + N_PREV * CHUNK
ROW_W = BIAS_W + Q_TILE
SOFTMAX_ROWS = 64
LOG2_E = math.log2(math.e)


def _act_table():
    s = math.sqrt(2.0 / math.pi)
    ident = (1.0, 0.0, 0.0, 0.0, 0.0, 0.0)
    silu = (0.5, 0.0, 0.5, 0.0, 0.5, 0.0)
    gelu = (0.5, 0.0, 0.5, 0.0, s, s * 0.044715)
    sigm = (0.0, 0.5, 0.0, 0.5, 0.5, 0.0)
    rows = [ident] * N_COL_BLOCKS
    rows[COL_Q] = (HEAD_DIM ** -0.5 * LOG2_E, 0.0, 0.0, 0.0, 0.0, 0.0)
    for c in (COL_GA, COL_GB, COL_GC):
        rows[c] = silu
    for c in (COL_U, COL_VC):
        rows[c] = gelu
    for c in range(COL_M, N_COL_BLOCKS):
        rows[c] = sigm
    return np.asarray(rows, np.float32)


def _rms_cast_kernel(x_ref, g_ref, h_ref):
    x = x_ref[...]
    r = lax.rsqrt(jnp.mean(x * x, axis=-1, keepdims=True) + EPS)
    h_ref[...] = (x * r * g_ref[...]).astype(BF16)


def _rms_cast(x2, g):
    n = x2.shape[0]
    return pl.pallas_call(
        _rms_cast_kernel,
        grid=(n // TM_NORM,),
        in_specs=[pl.BlockSpec((TM_NORM, D_MODEL), lambda i: (i, 0)),
                  pl.BlockSpec((1, D_MODEL), lambda i: (0, 0))],
        out_specs=pl.BlockSpec((TM_NORM, D_MODEL), lambda i: (i, 0)),
        out_shape=jax.ShapeDtypeStruct((n, D_MODEL), BF16),
        compiler_params=pltpu.CompilerParams(dimension_semantics=("arbitrary",),
                                             vmem_limit_bytes=VMEM_LIMIT),
        name="rms_cast",
    )(x2, g.reshape(1, D_MODEL))


def _in_proj_kernel(coef_ref, h_ref, w_ref, z_ref, wb_ref):
    j = pl.program_id(0)

    @pl.when(pl.program_id(1) == 0)
    def _():
        wb_ref[...] = w_ref[0].astype(BF16)

    a, b, c, d, k, l = (coef_ref[j, t] for t in range(6))
    for rows in (slice(r, r + SUB_M) for r in range(0, TM_IN, SUB_M)):
        for cols in (slice(n, n + SUB_N) for n in range(0, D_BR, SUB_N)):
            z = jnp.dot(h_ref[rows, :], wb_ref[:, cols], preferred_element_type=F32)
            t = jnp.tanh(z * (k + l * (z * z)))
            z_ref[rows, cols] = (a * z + b + (c * z + d) * t).astype(BF16)


def _in_proj(h2, w_in, layer):
    n = h2.shape[0]
    coef = jnp.asarray(_act_table())
    return pl.pallas_call(
        _in_proj_kernel,
        grid=(N_COL_BLOCKS, n // TM_IN),
        in_specs=[
            pl.BlockSpec(memory_space=pltpu.SMEM),
            pl.BlockSpec((TM_IN, D_MODEL), lambda j, i: (i, 0)),
            pl.BlockSpec((1, D_MODEL, D_BR), lambda j, i: (layer, 0, j)),
        ],
        out_specs=pl.BlockSpec((TM_IN, D_BR), lambda j, i: (i, j)),
        out_shape=jax.ShapeDtypeStruct((n, IN_COLS), BF16),
        scratch_shapes=[pltpu.VMEM((D_MODEL, D_BR), BF16)],
        compiler_params=pltpu.CompilerParams(
            dimension_semantics=("arbitrary", "arbitrary"),
            vmem_limit_bytes=VMEM_LIMIT),
        name="in_proj",
    )(coef, h2, w_in)


def _attn_kernel(q_ref, k_ref, v_ref, g_ref, rows_ref, o_ref, bias_ref):
    t = pl.program_id(1)
    start = pl.multiple_of(jnp.maximum(t * Q_TILE - N_PREV * CHUNK, 0), Q_TILE)
    shift = pl.multiple_of(jnp.maximum(N_PREV * CHUNK - t * Q_TILE, 0), LANES)

    @pl.when((pl.program_id(0) == 0) & (t == 0))
    def _():
        r = lax.broadcasted_iota(jnp.int32, (Q_TILE, BIAS_W), 0)
        band = lax.broadcasted_iota(jnp.int32, (Q_TILE, BIAS_W), 1)
        band_lo = jnp.where(r < CHUNK, 0, CHUNK)
        visible = (band >= band_lo) & (band < band_lo + (N_PREV + 1) * CHUNK)
        for h in range(N_HEADS):
            row = jnp.broadcast_to(rows_ref[h:h + 1, :], (Q_TILE, ROW_W))
            toe = pltpu.roll(row, ROW_W - Q_TILE, 1, stride=1, stride_axis=0)[:, :BIAS_W]
            bias_ref[h // 2, (h % 2) * Q_TILE:(h % 2 + 1) * Q_TILE, :] = jnp.where(visible, toe, NEG_INF)

    low_head = lax.broadcasted_iota(jnp.int32, (Q_TILE, LANES), 1) < HEAD_DIM
    col_slices = [slice(p * LANES, (p + 1) * LANES) for p in range(HEAD_PAIRS)]

    def scores(p):
        q = q_ref[0, :, col_slices[p]]
        zero = jnp.zeros_like(q)
        q2 = jnp.concatenate([jnp.where(low_head, q, zero), jnp.where(low_head, zero, q)], axis=0)
        kw = k_ref[0, pl.ds(start, WIN), col_slices[p]]
        return lax.dot_general(q2, kw, (((1,), (1,)), ((), ())), preferred_element_type=F32)

    def softmax_parts(p, s):
        e_parts, denom_parts = [], []
        for rows in (slice(r, r + SOFTMAX_ROWS) for r in range(0, 2 * Q_TILE, SOFTMAX_ROWS)):
            sr = s[rows] + bias_ref[p, rows, pl.ds(shift, WIN)]
            er = jnp.exp2(sr - jnp.max(sr, axis=-1, keepdims=True))
            denom_parts.append(jnp.sum(er, axis=-1, keepdims=True))
            e_parts.append(er.astype(BF16))
        return jnp.concatenate(e_parts, axis=0), jnp.concatenate(denom_parts, axis=0)

    def weighted_values(p, e, denom):
        vw = v_ref[0, pl.ds(start, WIN), col_slices[p]]
        r = jnp.dot(e, vw, preferred_element_type=F32) / denom
        o = jnp.where(low_head, r[:Q_TILE], r[Q_TILE:])
        o_ref[0, :, col_slices[p]] = (o * g_ref[0, :, col_slices[p]].astype(F32)).astype(BF16)

    s_next = scores(0)
    for p in range(HEAD_PAIRS):
        s = s_next
        if p + 1 < HEAD_PAIRS:
            s_next = scores(p + 1)
        weighted_values(p, *softmax_parts(p, s))


def _bias_rows(rel_bias):
    dist = N_PREV * CHUNK + Q_TILE - jnp.arange(ROW_W)
    idx = jnp.clip(dist, -(CHUNK - 1), MAX_REL) + (CHUNK - 1)
    return rel_bias[:, idx].astype(F32) * LOG2_E


def _attention(z3, bias_rows):
    b, s, _ = z3.shape
    return pl.pallas_call(
        _attn_kernel,
        grid=(b, s // Q_TILE),
        in_specs=[
            pl.BlockSpec((1, Q_TILE, D_BR), lambda i, t: (i, t, COL_Q)),
            pl.BlockSpec((1, s, D_BR), lambda i, t: (i, 0, COL_K)),
            pl.BlockSpec((1, s, D_BR), lambda i, t: (i, 0, COL_V)),
            pl.BlockSpec((1, Q_TILE, D_BR), lambda i, t: (i, t, COL_GA)),
            pl.BlockSpec((N_HEADS, ROW_W), lambda i, t: (0, 0)),
        ],
        out_specs=pl.BlockSpec((1, Q_TILE, D_BR), lambda i, t: (i, t, 0)),
        out_shape=jax.ShapeDtypeStruct((b, s, D_BR), BF16),
        scratch_shapes=[pltpu.VMEM((HEAD_PAIRS, 2 * Q_TILE, BIAS_W), F32)],
        compiler_params=pltpu.CompilerParams(
            dimension_semantics=("arbitrary", "arbitrary"),
            vmem_limit_bytes=VMEM_LIMIT),
        name="attn",
    )(z3, z3, z3, z3, bias_rows)


def _mix_out_kernel(tiles_per_seq, emit_h,
                    a_ref, bg_ref, cg_ref, hb_ref, cprev_ref, hprev_ref, sgb_ref,
                    u_ref, vc_ref, sgc_ref,
                    ga0_ref, ga1_ref, gb0_ref, gb1_ref, gc0_ref, gc1_ref,
                    x_ref, convw_ref, lng_ref, lnb_ref, spw_ref, spb_ref,
                    wa_ref, wb_ref, wc_ref, wout_ref, postg_ref, *rest):
    if emit_h:
        preg_ref, o_ref, h_ref = rest
    else:
        (o_ref,) = rest
    i = pl.program_id(0)
    tm = x_ref.shape[0]
    row = lax.broadcasted_iota(jnp.int32, (tm, D_BR), 0)

    ch = cg_ref[...].astype(F32) * hb_ref[...].astype(F32)
    prev = cprev_ref[...].astype(F32) * hprev_ref[...].astype(F32)
    prev = jnp.where(i % tiles_per_seq == 0, 0.0, prev)
    ch1 = jnp.where(row == 0, prev[HALO_ROWS - 1:HALO_ROWS], pltpu.roll(ch, 1, 0))
    ch2 = jnp.where(row == 0, prev[HALO_ROWS - 2:HALO_ROWS - 1],
                    jnp.where(row == 1, prev[HALO_ROWS - 1:HALO_ROWS], pltpu.roll(ch, 2, 0)))
    conv = convw_ref[0, 0:1, :] * ch2 + convw_ref[0, 1:2, :] * ch1 + convw_ref[0, 2:3, :] * ch
    ob = (bg_ref[...].astype(F32) * conv * sgb_ref[...].astype(F32)).astype(BF16)

    v = vc_ref[...].astype(F32)
    mu = jnp.mean(v, axis=-1, keepdims=True)
    vc = v - mu
    var = jnp.mean(vc * vc, axis=-1, keepdims=True)
    vn = (vc * lax.rsqrt(var + EPS) * lng_ref[...] + lnb_ref[...]).astype(BF16)
    tri = (lax.broadcasted_iota(jnp.int32, (SGU_CHUNK, SGU_CHUNK), 0)
           >= lax.broadcasted_iota(jnp.int32, (SGU_CHUNK, SGU_CHUNK), 1))
    oc_rows = []
    for n in range(tm // SGU_CHUNK):
        rows = slice(n * SGU_CHUNK, (n + 1) * SGU_CHUNK)
        oc_cols = []
        for g in range(N_GROUPS):
            cols = slice(g * GROUP_CH, (g + 1) * GROUP_CH)
            w = jnp.where(tri, spw_ref[0, g], 0.0).astype(BF16)
            mixed = jnp.dot(w, vn[rows, cols], preferred_element_type=F32) + spb_ref[g]
            oc_cols.append(mixed)
        oc_rows.append(jnp.concatenate(oc_cols, axis=1))
    mixed = jnp.concatenate(oc_rows, axis=0)
    oc = (u_ref[...].astype(F32) * mixed * sgc_ref[...].astype(F32)).astype(BF16)

    ya = jnp.dot(a_ref[...], wa_ref[0], preferred_element_type=F32)
    yb = jnp.dot(ob, wb_ref[0], preferred_element_type=F32)
    yc = jnp.dot(oc, wc_ref[0], preferred_element_type=F32)
    ga = jnp.concatenate([ga0_ref[...], ga1_ref[...]], axis=1).astype(F32)
    gb = jnp.concatenate([gb0_ref[...], gb1_ref[...]], axis=1).astype(F32)
    gc = jnp.concatenate([gc0_ref[...], gc1_ref[...]], axis=1).astype(F32)
    merged = (ga * ya + gb * yb + gc * yc).astype(BF16)
    y = jnp.dot(merged, wout_ref[0], preferred_element_type=F32)
    r = lax.rsqrt(jnp.mean(y * y, axis=-1, keepdims=True) + EPS)
    xn = x_ref[...] + y * r * postg_ref[...]
    o_ref[...] = xn
    if emit_h:
        rn = lax.rsqrt(jnp.mean(xn * xn, axis=-1, keepdims=True) + EPS)
        h_ref[...] = (xn * rn * preg_ref[...]).astype(BF16)


def _mix_out(a2, z2, x2, seq, layer, conv_w, ln_g, ln_b, sp_w, sp_b, wa, wb, wc, wout,
             post_g, pre_g_next):
    n = x2.shape[0]
    tm = TM_OUT
    emit_h = pre_g_next is not None
    halo_per_tile = tm // HALO_ROWS
    zblk = lambda col: pl.BlockSpec((tm, D_BR), lambda i, col=col: (i, col))
    halo = lambda col: pl.BlockSpec(
        (HALO_ROWS, D_BR), lambda i, col=col: (jnp.maximum(i * halo_per_tile - 1, 0), col))
    const = lambda shape: pl.BlockSpec(shape, lambda i: (0,) * len(shape),
                                       pipeline_mode=pl.Buffered(1))
    per_layer = lambda shape: pl.BlockSpec((1,) + shape, lambda i: (layer,) + (0,) * len(shape),
                                           pipeline_mode=pl.Buffered(1))
    row_tile = pl.BlockSpec((tm, D_MODEL), lambda i: (i, 0))
    in_specs = [
        pl.BlockSpec((tm, D_BR), lambda i: (i, 0)),
        zblk(COL_B), zblk(COL_C), zblk(COL_HB), halo(COL_C), halo(COL_HB), zblk(COL_GB),
        zblk(COL_U), zblk(COL_VC), zblk(COL_GC),
        zblk(COL_M), zblk(COL_M + 1), zblk(COL_M + 2), zblk(COL_M + 3),
        zblk(COL_M + 4), zblk(COL_M + 5),
        row_tile,
        per_layer((CONV_WIDTH, D_BR)), const((1, D_BR)), const((1, D_BR)),
        per_layer((N_GROUPS, SGU_CHUNK, SGU_CHUNK)), const((N_GROUPS, SGU_CHUNK, 1)),
        per_layer((D_BR, D_MODEL)), per_layer((D_BR, D_MODEL)), per_layer((D_BR, D_MODEL)),
        per_layer((D_MODEL, D_MODEL)), const((1, D_MODEL)),
    ]
    args = [a2, *([z2] * 15), x2, conv_w, ln_g.reshape(1, D_BR), ln_b.reshape(1, D_BR),
            sp_w, sp_b.reshape(N_GROUPS, SGU_CHUNK, 1), wa, wb, wc, wout,
            post_g.reshape(1, D_MODEL)]
    out_specs = row_tile
    out_shape = jax.ShapeDtypeStruct((n, D_MODEL), F32)
    if emit_h:
        in_specs.append(const((1, D_MODEL)))
        args.append(pre_g_next.reshape(1, D_MODEL))
        out_specs = (row_tile, row_tile)
        out_shape = (out_shape, jax.ShapeDtypeStruct((n, D_MODEL), BF16))
    return pl.pallas_call(
        functools.partial(_mix_out_kernel, seq // tm, emit_h),
        grid=(n // tm,),
        in_specs=in_specs,
        out_specs=out_specs,
        out_shape=out_shape,
        compiler_params=pltpu.CompilerParams(
            dimension_semantics=("arbitrary",),
            vmem_limit_bytes=VMEM_LIMIT),
        name="mix_out",
    )(*args)


def kernel(x, pre_norm, w_in, rel_bias, conv_w, sgu_ln_g, sgu_ln_b, spatial_w, spatial_b,
           w_branch_a, w_branch_b, w_branch_c, w_out, post_norm):
    b, s, d = x.shape
    n = b * s
    depth = pre_norm.shape[0]
    x2 = x.reshape(n, d)
    wa, wb, wc, wout = (w.astype(BF16) for w in (w_branch_a, w_branch_b, w_branch_c, w_out))
    h2 = _rms_cast(x2, pre_norm[0])
    for i in range(depth):
        z2 = _in_proj(h2, w_in, i)
        a = _attention(z2.reshape(b, s, IN_COLS), _bias_rows(rel_bias[i]))
        res = _mix_out(a.reshape(n, D_BR), z2, x2, s, i, conv_w, sgu_ln_g[i], sgu_ln_b[i],
                       spatial_w, spatial_b[i], wa, wb, wc, wout, post_norm[i],
                       pre_norm[i + 1] if i + 1 < depth else None)
        x2, h2 = res if i + 1 < depth else (res, None)
    return x2.reshape(b, s, d)
```

```python
import functools
import math

import jax
import jax.numpy as jnp
import numpy as np
from jax import lax
from jax.experimental import pallas as pl
from jax.experimental.pallas import tpu as pltpu

F32 = jnp.float32
BF16 = jnp.bfloat16

D_MODEL = 2048
CHUNK = 64
N_PREV = 8
D_BR = D_MODEL // 2
N_HEADS = 16
HEAD_DIM = D_BR // N_HEADS
MAX_REL = 256
CONV_WIDTH = 3
SGU_CHUNK = 128
N_GROUPS = 8
GROUP_CH = D_BR // N_GROUPS
EPS = 1e-6
NEG_INF = -1e30

N_COL_BLOCKS = 11 + 3 * (D_MODEL // D_BR)

COL_Q, COL_K, COL_V, COL_GA, COL_B, COL_C, COL_HB, COL_GB, COL_U, COL_VC, COL_GC, COL_M = range(12)

LANES = 128
HEAD_PAIRS = D_BR // LANES
HALO_ROWS = 16

TM_NORM = 512
TM_IN = 2048
SUB_M = 1024
TM_OUT = 256
VMEM_LIMIT = 56 * 1024 * 1024

Q_TILE = 2 * CHUNK
WIN = (N_PREV + 2) * CHUNK
BIAS_W = WIN + N_PREV * CHUNK
ROW_W = BIAS_W + Q_TILE
SOFTMAX_ROWS = 64
LOG2_E = math.log2(math.e)


GELU_K = math.sqrt(2.0 / math.pi)

LIN_BLOCKS = (COL_Q, COL_K, COL_V, COL_B, COL_C, COL_HB)
LIN_SCALES = (HEAD_DIM ** -0.5 * LOG2_E, 1.0, 1.0, 1.0, 1.0, 1.0)
SILU_BLOCKS = (COL_GA, COL_GB, COL_GC)
GELU_BLOCKS = (COL_U, COL_VC)
GATE_BLOCKS = tuple(range(COL_M, N_COL_BLOCKS))
LIN_Q, LIN_K, LIN_V, LIN_B, LIN_C, LIN_HB = range(6)
SILU_GA, SILU_GB, SILU_GC = range(3)
GELU_U, GELU_VC = range(2)


def _rms_cast_kernel(x_ref, g_ref, h_ref):
    x = x_ref[...]
    r = lax.rsqrt(jnp.mean(x * x, axis=-1, keepdims=True) + EPS)
    h_ref[...] = (x * r * g_ref[...]).astype(BF16)


def _rms_cast(x2, g):
    n = x2.shape[0]
    return pl.pallas_call(
        _rms_cast_kernel,
        grid=(n // TM_NORM,),
        in_specs=[pl.BlockSpec((TM_NORM, D_MODEL), lambda i: (i, 0)),
                  pl.BlockSpec((1, D_MODEL), lambda i: (0, 0))],
        out_specs=pl.BlockSpec((TM_NORM, D_MODEL), lambda i: (i, 0)),
        out_shape=jax.ShapeDtypeStruct((n, D_MODEL), BF16),
        compiler_params=pltpu.CompilerParams(dimension_semantics=("arbitrary",),
                                             vmem_limit_bytes=VMEM_LIMIT),
        name="rms_cast",
    )(x2, g.reshape(1, D_MODEL))


def _silu(z):
    hz = 0.5 * z
    return hz * jnp.tanh(hz) + hz


def _gelu_tanh(z):
    hz = 0.5 * z
    return hz * jnp.tanh(z * (GELU_K + (GELU_K * 0.044715) * (z * z))) + hz


def _sigmoid(z):
    return 0.5 * jnp.tanh(0.5 * z) + 0.5


def _in_proj_kernel(act, cols_ref, *refs):
    if act is None:
        scale_ref, *refs = refs
        scale = scale_ref[pl.program_id(0)]
        act = lambda z: z * scale
    h_ref, w_ref, z_ref, wb_ref = refs

    @pl.when(pl.program_id(1) == 0)
    def _():
        wb_ref[...] = w_ref[0].astype(BF16)

    for rows in (slice(r, r + SUB_M) for r in range(0, TM_IN, SUB_M)):
        z = jnp.dot(h_ref[rows, :], wb_ref[...], preferred_element_type=F32)
        z_ref[rows, :] = act(z).astype(BF16)


def _in_proj(name, h2, w_in, layer, col_blocks, act=None, scales=None):
    n = h2.shape[0]
    n_blocks = len(col_blocks)
    in_specs = [
        pl.BlockSpec((TM_IN, D_MODEL), lambda j, i, cols: (i, 0)),
        pl.BlockSpec((1, D_MODEL, D_BR), lambda j, i, cols: (layer, 0, cols[j])),
    ]
    args = [h2, w_in]
    if act is None:
        in_specs.insert(0, pl.BlockSpec(memory_space=pltpu.SMEM))
        args.insert(0, jnp.asarray(scales, F32))
    grid_spec = pltpu.PrefetchScalarGridSpec(
        num_scalar_prefetch=1,
        grid=(n_blocks, n // TM_IN),
        in_specs=in_specs,
        out_specs=pl.BlockSpec((TM_IN, D_BR), lambda j, i, cols: (i, j)),
        scratch_shapes=[pltpu.VMEM((D_MODEL, D_BR), BF16)],
    )
    return pl.pallas_call(
        functools.partial(_in_proj_kernel, act),
        grid_spec=grid_spec,
        out_shape=jax.ShapeDtypeStruct((n, n_blocks * D_BR), BF16),
        compiler_params=pltpu.CompilerParams(
            dimension_semantics=("arbitrary", "arbitrary"),
            vmem_limit_bytes=VMEM_LIMIT),
        name=name,
    )(jnp.asarray(col_blocks, jnp.int32), *args)


def _attn_kernel(q_ref, k_ref, v_ref, g_ref, rows_ref, o_ref, bias_ref):
    t = pl.program_id(1)
    start = pl.multiple_of(jnp.maximum(t * Q_TILE - N_PREV * CHUNK, 0), Q_TILE)
    shift = pl.multiple_of(jnp.maximum(N_PREV * CHUNK - t * Q_TILE, 0), LANES)

    @pl.when((pl.program_id(0) == 0) & (t == 0))
    def _():
        r = lax.broadcasted_iota(jnp.int32, (Q_TILE, BIAS_W), 0)
        band = lax.broadcasted_iota(jnp.int32, (Q_TILE, BIAS_W), 1)
        band_lo = jnp.where(r < CHUNK, 0, CHUNK)
        visible = (band >= band_lo) & (band < band_lo + (N_PREV + 1) * CHUNK)
        for h in range(N_HEADS):
            row = jnp.broadcast_to(rows_ref[h:h + 1, :], (Q_TILE, ROW_W))
            toe = pltpu.roll(row, ROW_W - Q_TILE, 1, stride=1, stride_axis=0)[:, :BIAS_W]
            bias_ref[h // 2, (h % 2) * Q_TILE:(h % 2 + 1) * Q_TILE, :] = jnp.where(visible, toe, NEG_INF)

    low_head = lax.broadcasted_iota(jnp.int32, (Q_TILE, LANES), 1) < HEAD_DIM
    col_slices = [slice(p * LANES, (p + 1) * LANES) for p in range(HEAD_PAIRS)]

    def scores(p):
        q = q_ref[0, :, col_slices[p]]
        zero = jnp.zeros_like(q)
        q2 = jnp.concatenate([jnp.where(low_head, q, zero), jnp.where(low_head, zero, q)], axis=0)
        kw = k_ref[0, pl.ds(start, WIN), col_slices[p]]
        return lax.dot_general(q2, kw, (((1,), (1,)), ((), ())), preferred_element_type=F32)

    def softmax_parts(p, s):
        e_parts, denom_parts = [], []
        for rows in (slice(r, r + SOFTMAX_ROWS) for r in range(0, 2 * Q_TILE, SOFTMAX_ROWS)):
            sr = s[rows] + bias_ref[p, rows, pl.ds(shift, WIN)]
            er = jnp.exp2(sr - jnp.max(sr, axis=-1, keepdims=True))
            denom_parts.append(jnp.sum(er, axis=-1, keepdims=True))
            e_parts.append(er.astype(BF16))
        return jnp.concatenate(e_parts, axis=0), jnp.concatenate(denom_parts, axis=0)

    def weighted_values(p, e, denom):
        vw = v_ref[0, pl.ds(start, WIN), col_slices[p]]
        r = jnp.dot(e, vw, preferred_element_type=F32) / denom
        o = jnp.where(low_head, r[:Q_TILE], r[Q_TILE:])
        o_ref[0, :, col_slices[p]] = (o * g_ref[0, :, col_slices[p]].astype(F32)).astype(BF16)

    s_next = scores(0)
    for p in range(HEAD_PAIRS):
        s = s_next
        if p + 1 < HEAD_PAIRS:
            s_next = scores(p + 1)
        weighted_values(p, *softmax_parts(p, s))


def _bias_rows(rel_bias):
    dist = N_PREV * CHUNK + Q_TILE - jnp.arange(ROW_W)
    idx = jnp.clip(dist, -(CHUNK - 1), MAX_REL) + (CHUNK - 1)
    return rel_bias[:, idx].astype(F32) * LOG2_E


def _attention(z_lin3, z_silu3, bias_rows):
    b, s, _ = z_lin3.shape
    return pl.pallas_call(
        _attn_kernel,
        grid=(b, s // Q_TILE),
        in_specs=[
            pl.BlockSpec((1, Q_TILE, D_BR), lambda i, t: (i, t, LIN_Q)),
            pl.BlockSpec((1, s, D_BR), lambda i, t: (i, 0, LIN_K)),
            pl.BlockSpec((1, s, D_BR), lambda i, t: (i, 0, LIN_V)),
            pl.BlockSpec((1, Q_TILE, D_BR), lambda i, t: (i, t, SILU_GA)),
            pl.BlockSpec((N_HEADS, ROW_W), lambda i, t: (0, 0)),
        ],
        out_specs=pl.BlockSpec((1, Q_TILE, D_BR), lambda i, t: (i, t, 0)),
        out_shape=jax.ShapeDtypeStruct((b, s, D_BR), BF16),
        scratch_shapes=[pltpu.VMEM((HEAD_PAIRS, 2 * Q_TILE, BIAS_W), F32)],
        compiler_params=pltpu.CompilerParams(
            dimension_semantics=("arbitrary", "arbitrary"),
            vmem_limit_bytes=VMEM_LIMIT),
        name="attn",
    )(z_lin3, z_lin3, z_lin3, z_silu3, bias_rows)


def _mix_out_kernel(tiles_per_seq, emit_h,
                    a_ref, bg_ref, cg_ref, hb_ref, cprev_ref, hprev_ref, sgb_ref,
                    u_ref, vc_ref, sgc_ref, ga_ref, gb_ref, gc_ref,
                    x_ref, convw_ref, lng_ref, lnb_ref, spw_ref, spb_ref,
                    wa_ref, wb_ref, wc_ref, wout_ref, postg_ref, *rest):
    if emit_h:
        preg_ref, o_ref, h_ref = rest
    else:
        (o_ref,) = rest
    i = pl.program_id(0)
    tm = x_ref.shape[0]
    row = lax.broadcasted_iota(jnp.int32, (tm, D_BR), 0)

    merged = ga_ref[...].astype(F32) * jnp.dot(a_ref[...], wa_ref[0], preferred_element_type=F32)

    ch = cg_ref[...].astype(F32) * hb_ref[...].astype(F32)
    prev = cprev_ref[...].astype(F32) * hprev_ref[...].astype(F32)
    prev = jnp.where(i % tiles_per_seq == 0, 0.0, prev)
    ch1 = jnp.where(row == 0, prev[HALO_ROWS - 1:HALO_ROWS], pltpu.roll(ch, 1, 0))
    ch2 = jnp.where(row == 0, prev[HALO_ROWS - 2:HALO_ROWS - 1],
                    jnp.where(row == 1, prev[HALO_ROWS - 1:HALO_ROWS], pltpu.roll(ch, 2, 0)))
    conv = convw_ref[0, 0:1, :] * ch2 + convw_ref[0, 1:2, :] * ch1 + convw_ref[0, 2:3, :] * ch
    ob = (bg_ref[...].astype(F32) * conv * sgb_ref[...].astype(F32)).astype(BF16)
    merged += gb_ref[...].astype(F32) * jnp.dot(ob, wb_ref[0], preferred_element_type=F32)

    v = vc_ref[...].astype(F32)
    mu = jnp.mean(v, axis=-1, keepdims=True)
    vc = v - mu
    var = jnp.mean(vc * vc, axis=-1, keepdims=True)
    vn = (vc * lax.rsqrt(var + EPS) * lng_ref[...] + lnb_ref[...]).astype(BF16)
    tri = (lax.broadcasted_iota(jnp.int32, (SGU_CHUNK, SGU_CHUNK), 0)
           >= lax.broadcasted_iota(jnp.int32, (SGU_CHUNK, SGU_CHUNK), 1))
    oc_rows = []
    for n in range(tm // SGU_CHUNK):
        rows = slice(n * SGU_CHUNK, (n + 1) * SGU_CHUNK)
        oc_cols = []
        for g in range(N_GROUPS):
            cols = slice(g * GROUP_CH, (g + 1) * GROUP_CH)
            w = jnp.where(tri, spw_ref[0, g], 0.0).astype(BF16)
            mixed = jnp.dot(w, vn[rows, cols], preferred_element_type=F32) + spb_ref[g]
            oc_cols.append(mixed)
        oc_rows.append(jnp.concatenate(oc_cols, axis=1))
    mixed = jnp.concatenate(oc_rows, axis=0)
    oc = (u_ref[...].astype(F32) * mixed * sgc_ref[...].astype(F32)).astype(BF16)

    merged += gc_ref[...].astype(F32) * jnp.dot(oc, wc_ref[0], preferred_element_type=F32)
    y = jnp.dot(merged.astype(BF16), wout_ref[0], preferred_element_type=F32)
    r = lax.rsqrt(jnp.mean(y * y, axis=-1, keepdims=True) + EPS)
    xn = x_ref[...] + y * r * postg_ref[...]
    o_ref[...] = xn
    if emit_h:
        rn = lax.rsqrt(jnp.mean(xn * xn, axis=-1, keepdims=True) + EPS)
        h_ref[...] = (xn * rn * preg_ref[...]).astype(BF16)


def _mix_out(a2, z_lin, z_silu, z_gelu, z_gate, x2, seq, layer, conv_w, ln_g, ln_b, sp_w, sp_b,
             wa, wb, wc, wout, post_g, pre_g_next):
    n = x2.shape[0]
    tm = TM_OUT
    emit_h = pre_g_next is not None
    halo_per_tile = tm // HALO_ROWS
    zblk = lambda col: pl.BlockSpec((tm, D_BR), lambda i, col=col: (i, col))
    gate = lambda branch: pl.BlockSpec((tm, D_MODEL), lambda i, branch=branch: (i, branch))
    halo = lambda col: pl.BlockSpec(
        (HALO_ROWS, D_BR), lambda i, col=col: (jnp.maximum(i * halo_per_tile - 1, 0), col))
    const = lambda shape: pl.BlockSpec(shape, lambda i: (0,) * len(shape),
                                       pipeline_mode=pl.Buffered(1))
    per_layer = lambda shape: pl.BlockSpec((1,) + shape, lambda i: (layer,) + (0,) * len(shape),
                                           pipeline_mode=pl.Buffered(1))
    row_tile = pl.BlockSpec((tm, D_MODEL), lambda i: (i, 0))
    in_specs = [
        pl.BlockSpec((tm, D_BR), lambda i: (i, 0)),
        zblk(LIN_B), zblk(LIN_C), zblk(LIN_HB), halo(LIN_C), halo(LIN_HB), zblk(SILU_GB),
        zblk(GELU_U), zblk(GELU_VC), zblk(SILU_GC),
        gate(0), gate(1), gate(2),
        row_tile,
        per_layer((CONV_WIDTH, D_BR)), const((1, D_BR)), const((1, D_BR)),
        per_layer((N_GROUPS, SGU_CHUNK, SGU_CHUNK)), const((N_GROUPS, SGU_CHUNK, 1)),
        per_layer((D_BR, D_MODEL)), per_layer((D_BR, D_MODEL)), per_layer((D_BR, D_MODEL)),
        per_layer((D_MODEL, D_MODEL)), const((1, D_MODEL)),
    ]
    args = [a2, z_lin, z_lin, z_lin, z_lin, z_lin, z_silu, z_gelu, z_gelu, z_silu,
            z_gate, z_gate, z_gate, x2, conv_w, ln_g.reshape(1, D_BR), ln_b.reshape(1, D_BR),
            sp_w, sp_b.reshape(N_GROUPS, SGU_CHUNK, 1), wa, wb, wc, wout,
            post_g.reshape(1, D_MODEL)]
    out_specs = row_tile
    out_shape = jax.ShapeDtypeStruct((n, D_MODEL), F32)
    if emit_h:
        in_specs.append(const((1, D_MODEL)))
        args.append(pre_g_next.reshape(1, D_MODEL))
        out_specs = (row_tile, row_tile)
        out_shape = (out_shape, jax.ShapeDtypeStruct((n, D_MODEL), BF16))
    return pl.pallas_call(
        functools.partial(_mix_out_kernel, seq // tm, emit_h),
        grid=(n // tm,),
        in_specs=in_specs,
        out_specs=out_specs,
        out_shape=out_shape,
        compiler_params=pltpu.CompilerParams(
            dimension_semantics=("arbitrary",),
            vmem_limit_bytes=VMEM_LIMIT),
        name="mix_out",
    )(*args)


def kernel(x, pre_norm, w_in, rel_bias, conv_w, sgu_ln_g, sgu_ln_b, spatial_w, spatial_b,
           w_branch_a, w_branch_b, w_branch_c, w_out, post_norm):
    b, s, d = x.shape
    n = b * s
    depth = pre_norm.shape[0]
    x2 = x.reshape(n, d)
    wa, wb, wc, wout = (w.astype(BF16) for w in (w_branch_a, w_branch_b, w_branch_c, w_out))
    h2 = _rms_cast(x2, pre_norm[0])
    for i in range(depth):
        z_lin = _in_proj("in_proj_lin", h2, w_in, i, LIN_BLOCKS, scales=LIN_SCALES)
        z_silu = _in_proj("in_proj_silu", h2, w_in, i, SILU_BLOCKS, act=_silu)
        z_gelu = _in_proj("in_proj_gelu", h2, w_in, i, GELU_BLOCKS, act=_gelu_tanh)
        z_gate = _in_proj("in_proj_gate", h2, w_in, i, GATE_BLOCKS, act=_sigmoid)
        a = _attention(z_lin.reshape(b, s, -1), z_silu.reshape(b, s, -1), _bias_rows(rel_bias[i]))
        res = _mix_out(a.reshape(n, D_BR), z_lin, z_silu, z_gelu, z_gate, x2, s, i,
                       conv_w, sgu_ln_g[i], sgu_ln_b[i], spatial_w, spatial_b[i],
                       wa, wb, wc, wout, post_norm[i],
                       pre_norm[i + 1] if i + 1 < depth else None)
        x2, h2 = res if i + 1 < depth else (res, None)
    return x2.reshape(b, s, d)
```

```python
import functools
import math

import jax
import jax.numpy as jnp
import numpy as np
from jax import lax
from jax.experimental import pallas as pl
from jax.experimental.pallas import tpu as pltpu

F32 = jnp.float32
BF16 = jnp.bfloat16

D_MODEL = 2048
CHUNK = 64
N_PREV = 8
D_BR = D_MODEL // 2
N_HEADS = 16
HEAD_DIM = D_BR // N_HEADS
MAX_REL = 256
CONV_WIDTH = 3
SGU_CHUNK = 128
N_GROUPS = 8
GROUP_CH = D_BR // N_GROUPS
EPS = 1e-6
NEG_INF = -1e30

N_COL_BLOCKS = 11 + 3 * (D_MODEL // D_BR)

COL_Q, COL_K, COL_V, COL_GA, COL_B, COL_C, COL_HB, COL_GB, COL_U, COL_VC, COL_GC, COL_M = range(12)

LANES = 128
HEAD_PAIRS = D_BR // LANES

TM_NORM = 512
TM_IN = 2048
SUB_M = 1024
TM_OUT = 256
VMEM_LIMIT = 56 * 1024 * 1024

Q_TILE = 2 * CHUNK
WIN = (N_PREV + 2) * CHUNK
BIAS_W = WIN + N_PREV * CHUNK
ROW_W = BIAS_W + Q_TILE
SOFTMAX_ROWS = 64
LOG2_E = math.log2(math.e)

GELU_K = math.sqrt(2.0 / math.pi)

LIN_BLOCKS = (COL_Q, COL_K, COL_V)
LIN_SCALES = (HEAD_DIM ** -0.5 * LOG2_E, 1.0, 1.0)
SILU_BLOCKS = (COL_GA, COL_GC)
GELU_BLOCKS = (COL_U, COL_VC)
GATE_BLOCKS = tuple(range(COL_M, N_COL_BLOCKS))
CONV_BLOCKS = (COL_C, COL_HB, COL_B, COL_GB)
LIN_Q, LIN_K, LIN_V = range(3)
SILU_GA, SILU_GC = range(2)
GELU_U, GELU_VC = range(2)
CONV_GROUP = D_BR // len(CONV_BLOCKS)


def _rms_cast_kernel(x_ref, g_ref, h_ref):
    x = x_ref[...]
    r = lax.rsqrt(jnp.mean(x * x, axis=-1, keepdims=True) + EPS)
    h_ref[...] = (x * r * g_ref[...]).astype(BF16)


def _rms_cast(x2, g):
    n = x2.shape[0]
    return pl.pallas_call(
        _rms_cast_kernel,
        grid=(n // TM_NORM,),
        in_specs=[pl.BlockSpec((TM_NORM, D_MODEL), lambda i: (i, 0)),
                  pl.BlockSpec((1, D_MODEL), lambda i: (0, 0))],
        out_specs=pl.BlockSpec((TM_NORM, D_MODEL), lambda i: (i, 0)),
        out_shape=jax.ShapeDtypeStruct((n, D_MODEL), BF16),
        compiler_params=pltpu.CompilerParams(dimension_semantics=("arbitrary",),
                                             vmem_limit_bytes=VMEM_LIMIT),
        name="rms_cast",
    )(x2, g.reshape(1, D_MODEL))


def _silu(z):
    hz = 0.5 * z
    return hz * jnp.tanh(hz) + hz


def _gelu_tanh(z):
    hz = 0.5 * z
    return hz * jnp.tanh(z * (GELU_K + (GELU_K * 0.044715) * (z * z))) + hz


def _sigmoid(z):
    return 0.5 * jnp.tanh(0.5 * z) + 0.5


def _in_proj_kernel(act, cols_ref, *refs):
    if act is None:
        scale_ref, *refs = refs
        scale = scale_ref[pl.program_id(0)]
        act = lambda z: z * scale
    h_ref, w_ref, z_ref, wb_ref = refs

    @pl.when(pl.program_id(1) == 0)
    def _():
        wb_ref[...] = w_ref[0].astype(BF16)

    for rows in (slice(r, r + SUB_M) for r in range(0, TM_IN, SUB_M)):
        z = jnp.dot(h_ref[rows, :], wb_ref[...], preferred_element_type=F32)
        z_ref[rows, :] = act(z).astype(BF16)


def _in_proj(name, h2, w_in, layer, col_blocks, act=None, scales=None):
    n = h2.shape[0]
    n_blocks = len(col_blocks)
    in_specs = [
        pl.BlockSpec((TM_IN, D_MODEL), lambda j, i, cols: (i, 0)),
        pl.BlockSpec((1, D_MODEL, D_BR), lambda j, i, cols: (layer, 0, cols[j])),
    ]
    args = [h2, w_in]
    if act is None:
        in_specs.insert(0, pl.BlockSpec(memory_space=pltpu.SMEM))
        args.insert(0, jnp.asarray(scales, F32))
    grid_spec = pltpu.PrefetchScalarGridSpec(
        num_scalar_prefetch=1,
        grid=(n_blocks, n // TM_IN),
        in_specs=in_specs,
        out_specs=pl.BlockSpec((TM_IN, D_BR), lambda j, i, cols: (i, j)),
        scratch_shapes=[pltpu.VMEM((D_MODEL, D_BR), BF16)],
    )
    return pl.pallas_call(
        functools.partial(_in_proj_kernel, act),
        grid_spec=grid_spec,
        out_shape=jax.ShapeDtypeStruct((n, n_blocks * D_BR), BF16),
        compiler_params=pltpu.CompilerParams(
            dimension_semantics=("arbitrary", "arbitrary"),
            vmem_limit_bytes=VMEM_LIMIT),
        name=name,
    )(jnp.asarray(col_blocks, jnp.int32), *args)


def _in_proj_conv_kernel(h_ref, *refs):
    w_refs, (convw_ref, ob_ref, wb_ref) = refs[:len(CONV_BLOCKS)], refs[len(CONV_BLOCKS):]

    @pl.when(pl.program_id(1) == 0)
    def _():
        for k, w_ref in enumerate(w_refs):
            wb_ref[:, k * CONV_GROUP:(k + 1) * CONV_GROUP] = w_ref[0].astype(BF16)

    row = lax.broadcasted_iota(jnp.int32, (SUB_M, CONV_GROUP), 0)
    taps = [convw_ref[0, k:k + 1, :] for k in range(CONV_WIDTH)]
    before1 = before2 = jnp.zeros((1, CONV_GROUP), F32)
    for rows in (slice(r, r + SUB_M) for r in range(0, TM_IN, SUB_M)):
        z = jnp.dot(h_ref[rows, :], wb_ref[...], preferred_element_type=F32)
        c, hb, b, g = (z[:, k * CONV_GROUP:(k + 1) * CONV_GROUP] for k in range(len(CONV_BLOCKS)))
        ch = c * hb
        ch1 = jnp.where(row == 0, before1, pltpu.roll(ch, 1, 0))
        ch2 = jnp.where(row == 0, before2, jnp.where(row == 1, before1, pltpu.roll(ch, 2, 0)))
        conv = taps[0] * ch2 + taps[1] * ch1 + taps[2] * ch
        ob_ref[rows, :] = (b * conv * _silu(g)).astype(BF16)
        before1, before2 = ch[SUB_M - 1:SUB_M], ch[SUB_M - 2:SUB_M - 1]


def _in_proj_conv(h2, w_in, conv_w, layer, seq):
    assert seq == TM_IN, "a conv tile must be one whole sequence"
    n = h2.shape[0]
    groups_per_block = D_BR // CONV_GROUP
    w_spec = lambda col: pl.BlockSpec(
        (1, D_MODEL, CONV_GROUP), lambda j, i, col=col: (layer, 0, col * groups_per_block + j))
    return pl.pallas_call(
        _in_proj_conv_kernel,
        grid=(groups_per_block, n // TM_IN),
        in_specs=[pl.BlockSpec((TM_IN, D_MODEL), lambda j, i: (i, 0)),
                  *[w_spec(col) for col in CONV_BLOCKS],
                  pl.BlockSpec((1, CONV_WIDTH, CONV_GROUP), lambda j, i: (layer, 0, j))],
        out_specs=pl.BlockSpec((TM_IN, CONV_GROUP), lambda j, i: (i, j)),
        out_shape=jax.ShapeDtypeStruct((n, D_BR), BF16),
        scratch_shapes=[pltpu.VMEM((D_MODEL, D_BR), BF16)],
        compiler_params=pltpu.CompilerParams(
            dimension_semantics=("arbitrary", "arbitrary"),
            vmem_limit_bytes=VMEM_LIMIT),
        name="in_proj_conv",
    )(h2, *([w_in] * len(CONV_BLOCKS)), conv_w)


def _attn_kernel(q_ref, k_ref, v_ref, g_ref, rows_ref, o_ref, bias_ref):
    t = pl.program_id(1)
    start = pl.multiple_of(jnp.maximum(t * Q_TILE - N_PREV * CHUNK, 0), Q_TILE)
    shift = pl.multiple_of(jnp.maximum(N_PREV * CHUNK - t * Q_TILE, 0), LANES)

    @pl.when((pl.program_id(0) == 0) & (t == 0))
    def _():
        r = lax.broadcasted_iota(jnp.int32, (Q_TILE, BIAS_W), 0)
        band = lax.broadcasted_iota(jnp.int32, (Q_TILE, BIAS_W), 1)
        band_lo = jnp.where(r < CHUNK, 0, CHUNK)
        visible = (band >= band_lo) & (band < band_lo + (N_PREV + 1) * CHUNK)
        for h in range(N_HEADS):
            row = jnp.broadcast_to(rows_ref[h:h + 1, :], (Q_TILE, ROW_W))
            toe = pltpu.roll(row, ROW_W - Q_TILE, 1, stride=1, stride_axis=0)[:, :BIAS_W]
            bias_ref[h // 2, (h % 2) * Q_TILE:(h % 2 + 1) * Q_TILE, :] = jnp.where(visible, toe, NEG_INF)

    low_head = lax.broadcasted_iota(jnp.int32, (Q_TILE, LANES), 1) < HEAD_DIM
    col_slices = [slice(p * LANES, (p + 1) * LANES) for p in range(HEAD_PAIRS)]

    def scores(p):
        q = q_ref[0, :, col_slices[p]]
        zero = jnp.zeros_like(q)
        q2 = jnp.concatenate([jnp.where(low_head, q, zero), jnp.where(low_head, zero, q)], axis=0)
        kw = k_ref[0, pl.ds(start, WIN), col_slices[p]]
        return lax.dot_general(q2, kw, (((1,), (1,)), ((), ())), preferred_element_type=F32)

    def softmax_parts(p, s):
        e_parts, denom_parts = [], []
        for rows in (slice(r, r + SOFTMAX_ROWS) for r in range(0, 2 * Q_TILE, SOFTMAX_ROWS)):
            sr = s[rows] + bias_ref[p, rows, pl.ds(shift, WIN)]
            er = jnp.exp2(sr - jnp.max(sr, axis=-1, keepdims=True))
            denom_parts.append(jnp.sum(er, axis=-1, keepdims=True))
            e_parts.append(er.astype(BF16))
        return jnp.concatenate(e_parts, axis=0), jnp.concatenate(denom_parts, axis=0)

    def weighted_values(p, e, denom):
        vw = v_ref[0, pl.ds(start, WIN), col_slices[p]]
        r = jnp.dot(e, vw, preferred_element_type=F32) / denom
        o = jnp.where(low_head, r[:Q_TILE], r[Q_TILE:])
        o_ref[0, :, col_slices[p]] = (o * g_ref[0, :, col_slices[p]].astype(F32)).astype(BF16)

    s_next = scores(0)
    for p in range(HEAD_PAIRS):
        s = s_next
        if p + 1 < HEAD_PAIRS:
            s_next = scores(p + 1)
        weighted_values(p, *softmax_parts(p, s))


def _bias_rows(rel_bias):
    dist = N_PREV * CHUNK + Q_TILE - jnp.arange(ROW_W)
    idx = jnp.clip(dist, -(CHUNK - 1), MAX_REL) + (CHUNK - 1)
    return rel_bias[:, idx].astype(F32) * LOG2_E


def _attention(z_lin3, z_silu3, bias_rows):
    b, s, _ = z_lin3.shape
    return pl.pallas_call(
        _attn_kernel,
        grid=(b, s // Q_TILE),
        in_specs=[
            pl.BlockSpec((1, Q_TILE, D_BR), lambda i, t: (i, t, LIN_Q)),
            pl.BlockSpec((1, s, D_BR), lambda i, t: (i, 0, LIN_K)),
            pl.BlockSpec((1, s, D_BR), lambda i, t: (i, 0, LIN_V)),
            pl.BlockSpec((1, Q_TILE, D_BR), lambda i, t: (i, t, SILU_GA)),
            pl.BlockSpec((N_HEADS, ROW_W), lambda i, t: (0, 0)),
        ],
        out_specs=pl.BlockSpec((1, Q_TILE, D_BR), lambda i, t: (i, t, 0)),
        out_shape=jax.ShapeDtypeStruct((b, s, D_BR), BF16),
        scratch_shapes=[pltpu.VMEM((HEAD_PAIRS, 2 * Q_TILE, BIAS_W), F32)],
        compiler_params=pltpu.CompilerParams(
            dimension_semantics=("arbitrary", "arbitrary"),
            vmem_limit_bytes=VMEM_LIMIT),
        name="attn",
    )(z_lin3, z_lin3, z_lin3, z_silu3, bias_rows)


def _mix_out_kernel(emit_h,
                    a0_ref, ga0_ref, a_ref, ga_ref, ob_ref, u_ref, vc_ref, sgc_ref, gb_ref, gc_ref,
                    x_ref, lng_ref, lnb_ref, spw_ref, spb_ref,
                    wa_ref, wb_ref, wc_ref, wout_ref, postg_ref, *rest):
    if emit_h:
        preg_ref, o_ref, h_ref, ya_ref = rest
    else:
        o_ref, ya_ref = rest
    tm = x_ref.shape[0]

    def gated_branch_a(act_ref, gate_ref):
        return gate_ref[...].astype(F32) * jnp.dot(act_ref[...], wa_ref[0], preferred_element_type=F32)

    @pl.when(pl.program_id(0) == 0)
    def _():
        ya_ref[...] = gated_branch_a(a0_ref, ga0_ref)

    merged = ya_ref[...]
    merged += gb_ref[...].astype(F32) * jnp.dot(ob_ref[...], wb_ref[0], preferred_element_type=F32)

    v = vc_ref[...].astype(F32)
    mu = jnp.mean(v, axis=-1, keepdims=True)
    vc = v - mu
    var = jnp.mean(vc * vc, axis=-1, keepdims=True)
    vn = (vc * lax.rsqrt(var + EPS) * lng_ref[...] + lnb_ref[...]).astype(BF16)
    tri = (lax.broadcasted_iota(jnp.int32, (SGU_CHUNK, SGU_CHUNK), 0)
           >= lax.broadcasted_iota(jnp.int32, (SGU_CHUNK, SGU_CHUNK), 1))
    oc_rows = []
    for n in range(tm // SGU_CHUNK):
        rows = slice(n * SGU_CHUNK, (n + 1) * SGU_CHUNK)
        oc_cols = []
        for g in range(N_GROUPS):
            cols = slice(g * GROUP_CH, (g + 1) * GROUP_CH)
            w = jnp.where(tri, spw_ref[0, g], 0.0).astype(BF16)
            mixed = jnp.dot(w, vn[rows, cols], preferred_element_type=F32) + spb_ref[g]
            oc_cols.append(mixed)
        oc_rows.append(jnp.concatenate(oc_cols, axis=1))
    mixed = jnp.concatenate(oc_rows, axis=0)
    oc = (u_ref[...].astype(F32) * mixed * sgc_ref[...].astype(F32)).astype(BF16)

    merged += gc_ref[...].astype(F32) * jnp.dot(oc, wc_ref[0], preferred_element_type=F32)
    y = jnp.dot(merged.astype(BF16), wout_ref[0], preferred_element_type=F32)
    ya_ref[...] = gated_branch_a(a_ref, ga_ref)
    r = lax.rsqrt(jnp.mean(y * y, axis=-1, keepdims=True) + EPS)
    xn = x_ref[...] + y * r * postg_ref[...]
    o_ref[...] = xn
    if emit_h:
        rn = lax.rsqrt(jnp.mean(xn * xn, axis=-1, keepdims=True) + EPS)
        h_ref[...] = (xn * rn * preg_ref[...]).astype(BF16)


def _mix_out(a2, ob2, z_silu, z_gelu, z_gate, x2, layer, ln_g, ln_b, sp_w, sp_b,
             wa, wb, wc, wout, post_g, pre_g_next):
    n = x2.shape[0]
    tm = TM_OUT
    emit_h = pre_g_next is not None
    n_tiles = n // tm
    ahead = lambda i: jnp.minimum(i + 1, n_tiles - 1)
    zblk = lambda col: pl.BlockSpec((tm, D_BR), lambda i, col=col: (i, col))
    gate = lambda branch: pl.BlockSpec((tm, D_MODEL), lambda i, branch=branch: (i, branch))
    const = lambda shape: pl.BlockSpec(shape, lambda i: (0,) * len(shape),
                                       pipeline_mode=pl.Buffered(1))
    per_layer = lambda shape: pl.BlockSpec((1,) + shape, lambda i: (layer,) + (0,) * len(shape),
                                           pipeline_mode=pl.Buffered(1))
    row_tile = pl.BlockSpec((tm, D_MODEL), lambda i: (i, 0))
    in_specs = [
        pl.BlockSpec((tm, D_BR), lambda i: (0, 0), pipeline_mode=pl.Buffered(1)),
        pl.BlockSpec((tm, D_MODEL), lambda i: (0, 0), pipeline_mode=pl.Buffered(1)),
        pl.BlockSpec((tm, D_BR), lambda i: (ahead(i), 0)),
        pl.BlockSpec((tm, D_MODEL), lambda i: (ahead(i), 0)),
        zblk(0),
        zblk(GELU_U), zblk(GELU_VC), zblk(SILU_GC),
        gate(1), gate(2),
        row_tile,
        const((1, D_BR)), const((1, D_BR)),
        per_layer((N_GROUPS, SGU_CHUNK, SGU_CHUNK)), const((N_GROUPS, SGU_CHUNK, 1)),
        per_layer((D_BR, D_MODEL)), per_layer((D_BR, D_MODEL)), per_layer((D_BR, D_MODEL)),
        per_layer((D_MODEL, D_MODEL)), const((1, D_MODEL)),
    ]
    args = [a2, z_gate, a2, z_gate, ob2, z_gelu, z_gelu, z_silu, z_gate, z_gate, x2,
            ln_g.reshape(1, D_BR), ln_b.reshape(1, D_BR),
            sp_w, sp_b.reshape(N_GROUPS, SGU_CHUNK, 1), wa, wb, wc, wout,
            post_g.reshape(1, D_MODEL)]
    out_specs = row_tile
    out_shape = jax.ShapeDtypeStruct((n, D_MODEL), F32)
    if emit_h:
        in_specs.append(const((1, D_MODEL)))
        args.append(pre_g_next.reshape(1, D_MODEL))
        out_specs = (row_tile, row_tile)
        out_shape = (out_shape, jax.ShapeDtypeStruct((n, D_MODEL), BF16))
    return pl.pallas_call(
        functools.partial(_mix_out_kernel, emit_h),
        grid=(n_tiles,),
        in_specs=in_specs,
        out_specs=out_specs,
        out_shape=out_shape,
        scratch_shapes=[pltpu.VMEM((tm, D_MODEL), F32)],
        compiler_params=pltpu.CompilerParams(
            dimension_semantics=("arbitrary",),
            vmem_limit_bytes=VMEM_LIMIT),
        name="mix_out",
    )(*args)


def kernel(x, pre_norm, w_in, rel_bias, conv_w, sgu_ln_g, sgu_ln_b, spatial_w, spatial_b,
           w_branch_a, w_branch_b, w_branch_c, w_out, post_norm):
    b, s, d = x.shape
    n = b * s
    depth = pre_norm.shape[0]
    x2 = x.reshape(n, d)
    wa, wb, wc, wout = (w.astype(BF16) for w in (w_branch_a, w_branch_b, w_branch_c, w_out))
    h2 = _rms_cast(x2, pre_norm[0])
    for i in range(depth):
        z_lin = _in_proj("in_proj_lin", h2, w_in, i, LIN_BLOCKS, scales=LIN_SCALES)
        z_silu = _in_proj("in_proj_silu", h2, w_in, i, SILU_BLOCKS, act=_silu)
        z_gelu = _in_proj("in_proj_gelu", h2, w_in, i, GELU_BLOCKS, act=_gelu_tanh)
        z_gate = _in_proj("in_proj_gate", h2, w_in, i, GATE_BLOCKS, act=_sigmoid)
        ob = _in_proj_conv(h2, w_in, conv_w, i, s)
        a = _attention(z_lin.reshape(b, s, -1), z_silu.reshape(b, s, -1), _bias_rows(rel_bias[i]))
        res = _mix_out(a.reshape(n, D_BR), ob, z_silu, z_gelu, z_gate, x2, i,
                       sgu_ln_g[i], sgu_ln_b[i], spatial_w, spatial_b[i],
                       wa, wb, wc, wout, post_norm[i],
                       pre_norm[i + 1] if i + 1 < depth else None)
        x2, h2 = res if i + 1 < depth else (res, None)
    return x2.reshape(b, s, d)
```

```python
import functools
import math

import jax
import jax.numpy as jnp
import numpy as np
from jax import lax
from jax.experimental import pallas as pl
from jax.experimental.pallas import tpu as pltpu

F32 = jnp.float32
BF16 = jnp.bfloat16

D_MODEL = 2048
CHUNK = 64
N_PREV = 8
D_BR = D_MODEL // 2
N_HEADS = 16
HEAD_DIM = D_BR // N_HEADS
MAX_REL = 256
CONV_WIDTH = 3
SGU_CHUNK = 128
N_GROUPS = 8
GROUP_CH = D_BR // N_GROUPS
EPS = 1e-6
NEG_INF = -1e30

N_COL_BLOCKS = 11 + 3 * (D_MODEL // D_BR)

COL_Q, COL_K, COL_V, COL_GA, COL_B, COL_C, COL_HB, COL_GB, COL_U, COL_VC, COL_GC, COL_M = range(12)

LANES = 128
HEAD_PAIRS = D_BR // LANES

TM_NORM = 512
TM_IN = 2048
SUB_M = 1024
TM_OUT = 256
VMEM_LIMIT = 56 * 1024 * 1024

Q_TILE = 2 * CHUNK
WIN = (N_PREV + 2) * CHUNK
BIAS_W = WIN + N_PREV * CHUNK
ROW_W = BIAS_W + Q_TILE
SOFTMAX_ROWS = 64
LOG2_E = math.log2(math.e)

GELU_K = math.sqrt(2.0 / math.pi)

LIN_BLOCKS = (COL_Q, COL_K, COL_V)
LIN_SCALES = (HEAD_DIM ** -0.5 * LOG2_E, 1.0, 1.0)
SILU_BLOCKS = (COL_GA, COL_GC)
GELU_BLOCKS = (COL_U, COL_VC)
GATE_BLOCKS = tuple(range(COL_M, N_COL_BLOCKS))
CONV_BLOCKS = (COL_C, COL_HB, COL_B, COL_GB)
LIN_Q, LIN_K, LIN_V = range(3)
SILU_GA, SILU_GC = range(2)
GELU_U, GELU_VC = range(2)
CONV_GROUP = D_BR // len(CONV_BLOCKS)


def _rms_cast_kernel(x_ref, g_ref, h_ref):
    x = x_ref[...]
    r = lax.rsqrt(jnp.mean(x * x, axis=-1, keepdims=True) + EPS)
    h_ref[...] = (x * r * g_ref[...]).astype(BF16)


def _rms_cast(x2, g):
    n = x2.shape[0]
    return pl.pallas_call(
        _rms_cast_kernel,
        grid=(n // TM_NORM,),
        in_specs=[pl.BlockSpec((TM_NORM, D_MODEL), lambda i: (i, 0)),
                  pl.BlockSpec((1, D_MODEL), lambda i: (0, 0))],
        out_specs=pl.BlockSpec((TM_NORM, D_MODEL), lambda i: (i, 0)),
        out_shape=jax.ShapeDtypeStruct((n, D_MODEL), BF16),
        compiler_params=pltpu.CompilerParams(dimension_semantics=("arbitrary",),
                                             vmem_limit_bytes=VMEM_LIMIT),
        name="rms_cast",
    )(x2, g.reshape(1, D_MODEL))


def _silu(z):
    hz = 0.5 * z
    return hz * jnp.tanh(hz) + hz


def _gelu_tanh(z):
    hz = 0.5 * z
    return hz * jnp.tanh(z * (GELU_K + (GELU_K * 0.044715) * (z * z))) + hz


def _sigmoid(z):
    return 0.5 * jnp.tanh(0.5 * z) + 0.5


def _in_proj_kernel(act, cols_ref, *refs):
    if act is None:
        scale_ref, *refs = refs
        scale = scale_ref[pl.program_id(0)]
        act = lambda z: z * scale
    h_ref, w_ref, z_ref, wb_ref = refs

    @pl.when(pl.program_id(1) == 0)
    def _():
        wb_ref[...] = w_ref[0].astype(BF16)

    for rows in (slice(r, r + SUB_M) for r in range(0, TM_IN, SUB_M)):
        z = jnp.dot(h_ref[rows, :], wb_ref[...], preferred_element_type=F32)
        z_ref[rows, :] = act(z).astype(BF16)


def _in_proj(name, h2, w_in, layer, col_blocks, act=None, scales=None):
    n = h2.shape[0]
    n_blocks = len(col_blocks)
    in_specs = [
        pl.BlockSpec((TM_IN, D_MODEL), lambda j, i, cols: (i, 0)),
        pl.BlockSpec((1, D_MODEL, D_BR), lambda j, i, cols: (layer, 0, cols[j])),
    ]
    args = [h2, w_in]
    if act is None:
        in_specs.insert(0, pl.BlockSpec(memory_space=pltpu.SMEM))
        args.insert(0, jnp.asarray(scales, F32))
    grid_spec = pltpu.PrefetchScalarGridSpec(
        num_scalar_prefetch=1,
        grid=(n_blocks, n // TM_IN),
        in_specs=in_specs,
        out_specs=pl.BlockSpec((TM_IN, D_BR), lambda j, i, cols: (i, j)),
        scratch_shapes=[pltpu.VMEM((D_MODEL, D_BR), BF16)],
    )
    return pl.pallas_call(
        functools.partial(_in_proj_kernel, act),
        grid_spec=grid_spec,
        out_shape=jax.ShapeDtypeStruct((n, n_blocks * D_BR), BF16),
        compiler_params=pltpu.CompilerParams(
            dimension_semantics=("arbitrary", "arbitrary"),
            vmem_limit_bytes=VMEM_LIMIT),
        name=name,
    )(jnp.asarray(col_blocks, jnp.int32), *args)


def _in_proj_conv_kernel(h_ref, *refs):
    w_refs, (convw_ref, ob_ref, wb_ref) = refs[:len(CONV_BLOCKS)], refs[len(CONV_BLOCKS):]

    @pl.when(pl.program_id(1) == 0)
    def _():
        for k, w_ref in enumerate(w_refs):
            wb_ref[:, k * CONV_GROUP:(k + 1) * CONV_GROUP] = w_ref[0].astype(BF16)

    row = lax.broadcasted_iota(jnp.int32, (SUB_M, CONV_GROUP), 0)
    taps = [convw_ref[0, k:k + 1, :] for k in range(CONV_WIDTH)]
    before1 = before2 = jnp.zeros((1, CONV_GROUP), F32)
    for rows in (slice(r, r + SUB_M) for r in range(0, TM_IN, SUB_M)):
        z = jnp.dot(h_ref[rows, :], wb_ref[...], preferred_element_type=F32)
        c, hb, b, g = (z[:, k * CONV_GROUP:(k + 1) * CONV_GROUP] for k in range(len(CONV_BLOCKS)))
        ch = c * hb
        ch1 = jnp.where(row == 0, before1, pltpu.roll(ch, 1, 0))
        ch2 = jnp.where(row == 0, before2, jnp.where(row == 1, before1, pltpu.roll(ch, 2, 0)))
        conv = taps[0] * ch2 + taps[1] * ch1 + taps[2] * ch
        ob_ref[rows, :] = (b * conv * _silu(g)).astype(BF16)
        before1, before2 = ch[SUB_M - 1:SUB_M], ch[SUB_M - 2:SUB_M - 1]


def _in_proj_conv(h2, w_in, conv_w, layer, seq):
    assert seq == TM_IN, "a conv tile must be one whole sequence"
    n = h2.shape[0]
    groups_per_block = D_BR // CONV_GROUP
    w_spec = lambda col: pl.BlockSpec(
        (1, D_MODEL, CONV_GROUP), lambda j, i, col=col: (layer, 0, col * groups_per_block + j))
    return pl.pallas_call(
        _in_proj_conv_kernel,
        grid=(groups_per_block, n // TM_IN),
        in_specs=[pl.BlockSpec((TM_IN, D_MODEL), lambda j, i: (i, 0)),
                  *[w_spec(col) for col in CONV_BLOCKS],
                  pl.BlockSpec((1, CONV_WIDTH, CONV_GROUP), lambda j, i: (layer, 0, j))],
        out_specs=pl.BlockSpec((TM_IN, CONV_GROUP), lambda j, i: (i, j)),
        out_shape=jax.ShapeDtypeStruct((n, D_BR), BF16),
        scratch_shapes=[pltpu.VMEM((D_MODEL, D_BR), BF16)],
        compiler_params=pltpu.CompilerParams(
            dimension_semantics=("arbitrary", "arbitrary"),
            vmem_limit_bytes=VMEM_LIMIT),
        name="in_proj_conv",
    )(h2, *([w_in] * len(CONV_BLOCKS)), conv_w)


def _attn_kernel(q_ref, k_ref, v_ref, g_ref, rows_ref, o_ref, bias_ref):
    @pl.when(pl.program_id(0) == 0)
    def _():
        r = lax.broadcasted_iota(jnp.int32, (Q_TILE, BIAS_W), 0)
        band = lax.broadcasted_iota(jnp.int32, (Q_TILE, BIAS_W), 1)
        band_lo = jnp.where(r < CHUNK, 0, CHUNK)
        visible = (band >= band_lo) & (band < band_lo + (N_PREV + 1) * CHUNK)
        for h in range(N_HEADS):
            row = jnp.broadcast_to(rows_ref[h:h + 1, :], (Q_TILE, ROW_W))
            toe = pltpu.roll(row, ROW_W - Q_TILE, 1, stride=1, stride_axis=0)[:, :BIAS_W]
            bias_ref[h // 2, (h % 2) * Q_TILE:(h % 2 + 1) * Q_TILE, :] = jnp.where(visible, toe, NEG_INF)

    low_head = lax.broadcasted_iota(jnp.int32, (Q_TILE, LANES), 1) < HEAD_DIM
    col_slices = [slice(p * LANES, (p + 1) * LANES) for p in range(HEAD_PAIRS)]

    def tile(t, carry):
        queries = pl.ds(pl.multiple_of(t * Q_TILE, Q_TILE), Q_TILE)
        start = pl.multiple_of(jnp.maximum(t * Q_TILE - N_PREV * CHUNK, 0), Q_TILE)
        shift = pl.multiple_of(jnp.maximum(N_PREV * CHUNK - t * Q_TILE, 0), LANES)

        def scores(p):
            q = q_ref[0, queries, col_slices[p]]
            zero = jnp.zeros_like(q)
            q2 = jnp.concatenate([jnp.where(low_head, q, zero), jnp.where(low_head, zero, q)], axis=0)
            kw = k_ref[0, pl.ds(start, WIN), col_slices[p]]
            return lax.dot_general(q2, kw, (((1,), (1,)), ((), ())), preferred_element_type=F32)

        def softmax_parts(p, s):
            e_parts, denom_parts = [], []
            for rows in (slice(r, r + SOFTMAX_ROWS) for r in range(0, 2 * Q_TILE, SOFTMAX_ROWS)):
                sr = s[rows] + bias_ref[p, rows, pl.ds(shift, WIN)]
                er = jnp.exp2(sr - jnp.max(sr, axis=-1, keepdims=True))
                denom_parts.append(jnp.sum(er, axis=-1, keepdims=True))
                e_parts.append(er.astype(BF16))
            return jnp.concatenate(e_parts, axis=0), jnp.concatenate(denom_parts, axis=0)

        def weighted_values(p, e, denom):
            vw = v_ref[0, pl.ds(start, WIN), col_slices[p]]
            r = jnp.dot(e, vw, preferred_element_type=F32) / denom
            o = jnp.where(low_head, r[:Q_TILE], r[Q_TILE:])
            gate = g_ref[0, queries, col_slices[p]].astype(F32)
            o_ref[0, queries, col_slices[p]] = (o * gate).astype(BF16)

        s_next = scores(0)
        for p in range(HEAD_PAIRS):
            s = s_next
            if p + 1 < HEAD_PAIRS:
                s_next = scores(p + 1)
            weighted_values(p, *softmax_parts(p, s))
        return carry

    lax.fori_loop(0, q_ref.shape[1] // Q_TILE, tile, 0)


def _bias_rows(rel_bias):
    dist = N_PREV * CHUNK + Q_TILE - jnp.arange(ROW_W)
    idx = jnp.clip(dist, -(CHUNK - 1), MAX_REL) + (CHUNK - 1)
    return rel_bias[:, idx].astype(F32) * LOG2_E


def _attention(z_lin3, z_silu3, bias_rows):
    b, s, _ = z_lin3.shape
    return pl.pallas_call(
        _attn_kernel,
        grid=(b,),
        in_specs=[
            pl.BlockSpec((1, s, D_BR), lambda i: (i, 0, LIN_Q)),
            pl.BlockSpec((1, s, D_BR), lambda i: (i, 0, LIN_K)),
            pl.BlockSpec((1, s, D_BR), lambda i: (i, 0, LIN_V)),
            pl.BlockSpec((1, s, D_BR), lambda i: (i, 0, SILU_GA)),
            pl.BlockSpec((N_HEADS, ROW_W), lambda i: (0, 0)),
        ],
        out_specs=pl.BlockSpec((1, s, D_BR), lambda i: (i, 0, 0)),
        out_shape=jax.ShapeDtypeStruct((b, s, D_BR), BF16),
        scratch_shapes=[pltpu.VMEM((HEAD_PAIRS, 2 * Q_TILE, BIAS_W), F32)],
        compiler_params=pltpu.CompilerParams(
            dimension_semantics=("arbitrary",),
            vmem_limit_bytes=VMEM_LIMIT),
        name="attn",
    )(z_lin3, z_lin3, z_lin3, z_silu3, bias_rows)


def _mix_out_kernel(emit_h,
                    a0_ref, ga0_ref, a_ref, ga_ref, ob_ref, u_ref, vc_ref, sgc_ref, gb_ref, gc_ref,
                    x_ref, lng_ref, lnb_ref, spw_ref, spb_ref,
                    wa_ref, wb_ref, wc_ref, wout_ref, postg_ref, *rest):
    if emit_h:
        preg_ref, o_ref, h_ref, ya_ref = rest
    else:
        o_ref, ya_ref = rest
    tm = x_ref.shape[0]

    def gated_branch_a(act_ref, gate_ref):
        return gate_ref[...].astype(F32) * jnp.dot(act_ref[...], wa_ref[0], preferred_element_type=F32)

    @pl.when(pl.program_id(0) == 0)
    def _():
        ya_ref[...] = gated_branch_a(a0_ref, ga0_ref)

    merged = ya_ref[...]
    merged += gb_ref[...].astype(F32) * jnp.dot(ob_ref[...], wb_ref[0], preferred_element_type=F32)

    v = vc_ref[...].astype(F32)
    mu = jnp.mean(v, axis=-1, keepdims=True)
    vc = v - mu
    var = jnp.mean(vc * vc, axis=-1, keepdims=True)
    vn = (vc * lax.rsqrt(var + EPS) * lng_ref[...] + lnb_ref[...]).astype(BF16)
    tri = (lax.broadcasted_iota(jnp.int32, (SGU_CHUNK, SGU_CHUNK), 0)
           >= lax.broadcasted_iota(jnp.int32, (SGU_CHUNK, SGU_CHUNK), 1))
    oc_rows = []
    for n in range(tm // SGU_CHUNK):
        rows = slice(n * SGU_CHUNK, (n + 1) * SGU_CHUNK)
        oc_cols = []
        for g in range(N_GROUPS):
            cols = slice(g * GROUP_CH, (g + 1) * GROUP_CH)
            w = jnp.where(tri, spw_ref[0, g], 0.0).astype(BF16)
            mixed = jnp.dot(w, vn[rows, cols], preferred_element_type=F32) + spb_ref[g]
            oc_cols.append(mixed)
        oc_rows.append(jnp.concatenate(oc_cols, axis=1))
    mixed = jnp.concatenate(oc_rows, axis=0)
    oc = (u_ref[...].astype(F32) * mixed * sgc_ref[...].astype(F32)).astype(BF16)

    merged += gc_ref[...].astype(F32) * jnp.dot(oc, wc_ref[0], preferred_element_type=F32)
    y = jnp.dot(merged.astype(BF16), wout_ref[0], preferred_element_type=F32)
    ya_ref[...] = gated_branch_a(a_ref, ga_ref)
    r = lax.rsqrt(jnp.mean(y * y, axis=-1, keepdims=True) + EPS)
    xn = x_ref[...] + y * r * postg_ref[...]
    o_ref[...] = xn
    if emit_h:
        rn = lax.rsqrt(jnp.mean(xn * xn, axis=-1, keepdims=True) + EPS)
        h_ref[...] = (xn * rn * preg_ref[...]).astype(BF16)


def _mix_out(a2, ob2, z_silu, z_gelu, z_gate, x2, layer, ln_g, ln_b, sp_w, sp_b,
             wa, wb, wc, wout, post_g, pre_g_next):
    n = x2.shape[0]
    tm = TM_OUT
    emit_h = pre_g_next is not None
    n_tiles = n // tm
    ahead = lambda i: jnp.minimum(i + 1, n_tiles - 1)
    zblk = lambda col: pl.BlockSpec((tm, D_BR), lambda i, col=col: (i, col))
    gate = lambda branch: pl.BlockSpec((tm, D_MODEL), lambda i, branch=branch: (i, branch))
    const = lambda shape: pl.BlockSpec(shape, lambda i: (0,) * len(shape),
                                       pipeline_mode=pl.Buffered(1))
    per_layer = lambda shape: pl.BlockSpec((1,) + shape, lambda i: (layer,) + (0,) * len(shape),
                                           pipeline_mode=pl.Buffered(1))
    row_tile = pl.BlockSpec((tm, D_MODEL), lambda i: (i, 0))
    in_specs = [
        pl.BlockSpec((tm, D_BR), lambda i: (0, 0), pipeline_mode=pl.Buffered(1)),
        pl.BlockSpec((tm, D_MODEL), lambda i: (0, 0), pipeline_mode=pl.Buffered(1)),
        pl.BlockSpec((tm, D_BR), lambda i: (ahead(i), 0)),
        pl.BlockSpec((tm, D_MODEL), lambda i: (ahead(i), 0)),
        zblk(0),
        zblk(GELU_U), zblk(GELU_VC), zblk(SILU_GC),
        gate(1), gate(2),
        row_tile,
        const((1, D_BR)), const((1, D_BR)),
        per_layer((N_GROUPS, SGU_CHUNK, SGU_CHUNK)), const((N_GROUPS, SGU_CHUNK, 1)),
        per_layer((D_BR, D_MODEL)), per_layer((D_BR, D_MODEL)), per_layer((D_BR, D_MODEL)),
        per_layer((D_MODEL, D_MODEL)), const((1, D_MODEL)),
    ]
    args = [a2, z_gate, a2, z_gate, ob2, z_gelu, z_gelu, z_silu, z_gate, z_gate, x2,
            ln_g.reshape(1, D_BR), ln_b.reshape(1, D_BR),
            sp_w, sp_b.reshape(N_GROUPS, SGU_CHUNK, 1), wa, wb, wc, wout,
            post_g.reshape(1, D_MODEL)]
    out_specs = row_tile
    out_shape = jax.ShapeDtypeStruct((n, D_MODEL), F32)
    if emit_h:
        in_specs.append(const((1, D_MODEL)))
        args.append(pre_g_next.reshape(1, D_MODEL))
        out_specs = (row_tile, row_tile)
        out_shape = (out_shape, jax.ShapeDtypeStruct((n, D_MODEL), BF16))
    return pl.pallas_call(
        functools.partial(_mix_out_kernel, emit_h),
        grid=(n_tiles,),
        in_specs=in_specs,
        out_specs=out_specs,
        out_shape=out_shape,
        scratch_shapes=[pltpu.VMEM((tm, D_MODEL), F32)],
        compiler_params=pltpu.CompilerParams(
            dimension_semantics=("arbitrary",),
            vmem_limit_bytes=VMEM_LIMIT),
        name="mix_out",
    )(*args)


def kernel(x, pre_norm, w_in, rel_bias, conv_w, sgu_ln_g, sgu_ln_b, spatial_w, spatial_b,
           w_branch_a, w_branch_b, w_branch_c, w_out, post_norm):
    b, s, d = x.shape
    n = b * s
    depth = pre_norm.shape[0]
    x2 = x.reshape(n, d)
    wa, wb, wc, wout = (w.astype(BF16) for w in (w_branch_a, w_branch_b, w_branch_c, w_out))
    h2 = _rms_cast(x2, pre_norm[0])
    for i in range(depth):
        z_lin = _in_proj("in_proj_lin", h2, w_in, i, LIN_BLOCKS, scales=LIN_SCALES)
        z_silu = _in_proj("in_proj_silu", h2, w_in, i, SILU_BLOCKS, act=_silu)
        z_gelu = _in_proj("in_proj_gelu", h2, w_in, i, GELU_BLOCKS, act=_gelu_tanh)
        z_gate = _in_proj("in_proj_gate", h2, w_in, i, GATE_BLOCKS, act=_sigmoid)
        ob = _in_proj_conv(h2, w_in, conv_w, i, s)
        a = _attention(z_lin.reshape(b, s, -1), z_silu.reshape(b, s, -1), _bias_rows(rel_bias[i]))
        res = _mix_out(a.reshape(n, D_BR), ob, z_silu, z_gelu, z_gate, x2, i,
                       sgu_ln_g[i], sgu_ln_b[i], spatial_w, spatial_b[i],
                       wa, wb, wc, wout, post_norm[i],
                       pre_norm[i + 1] if i + 1 < depth else None)
        x2, h2 = res if i + 1 < depth else (res, None)
    return x2.reshape(b, s, d)
```

```python
import functools
import math

import jax
import jax.numpy as jnp
import numpy as np
from jax import lax
from jax.experimental import pallas as pl
from jax.experimental.pallas import tpu as pltpu

F32 = jnp.float32
BF16 = jnp.bfloat16

D_MODEL = 2048
CHUNK = 64
N_PREV = 8
D_BR = D_MODEL // 2
N_HEADS = 16
HEAD_DIM = D_BR // N_HEADS
MAX_REL = 256
CONV_WIDTH = 3
SGU_CHUNK = 128
N_GROUPS = 8
GROUP_CH = D_BR // N_GROUPS
EPS = 1e-6
NEG_INF = -1e30

N_COL_BLOCKS = 11 + 3 * (D_MODEL // D_BR)

COL_Q, COL_K, COL_V, COL_GA, COL_B, COL_C, COL_HB, COL_GB, COL_U, COL_VC, COL_GC, COL_M = range(12)

LANES = 128
HEAD_PAIRS = D_BR // LANES

TM_NORM = 512
TM_IN = 2048
SUB_M = 1024
TM_OUT = 256
VMEM_LIMIT = 56 * 1024 * 1024

Q_TILE = 2 * CHUNK
WIN = (N_PREV + 2) * CHUNK
BIAS_W = WIN + N_PREV * CHUNK
ROW_W = BIAS_W + Q_TILE
SOFTMAX_ROWS = 64
LOG2_E = math.log2(math.e)

GELU_K = math.sqrt(2.0 / math.pi)

KIND_LINEAR, KIND_SILU, KIND_GELU, KIND_SIGMOID = range(4)
Z_BLOCKS = tuple(range(COL_M, N_COL_BLOCKS)) + (COL_Q, COL_K, COL_V, COL_GA, COL_GC, COL_U, COL_VC)
Z_KINDS = (KIND_SIGMOID,) * 6 + (KIND_LINEAR,) * 3 + (KIND_SILU,) * 2 + (KIND_GELU,) * 2
Z_SCALES = tuple(HEAD_DIM ** -0.5 * LOG2_E if c == COL_Q else 1.0 for c in Z_BLOCKS)
Z_Q, Z_K, Z_V, Z_GA, Z_GC, Z_U, Z_VC = range(6, 13)
CONV_BLOCKS = (COL_C, COL_HB, COL_B, COL_GB)
CONV_GROUP = D_BR // len(CONV_BLOCKS)


def _rms_cast_kernel(x_ref, g_ref, h_ref):
    x = x_ref[...]
    r = lax.rsqrt(jnp.mean(x * x, axis=-1, keepdims=True) + EPS)
    h_ref[...] = (x * r * g_ref[...]).astype(BF16)


def _rms_cast(x2, g):
    n = x2.shape[0]
    return pl.pallas_call(
        _rms_cast_kernel,
        grid=(n // TM_NORM,),
        in_specs=[pl.BlockSpec((TM_NORM, D_MODEL), lambda i: (i, 0)),
                  pl.BlockSpec((1, D_MODEL), lambda i: (0, 0))],
        out_specs=pl.BlockSpec((TM_NORM, D_MODEL), lambda i: (i, 0)),
        out_shape=jax.ShapeDtypeStruct((n, D_MODEL), BF16),
        compiler_params=pltpu.CompilerParams(dimension_semantics=("arbitrary",),
                                             vmem_limit_bytes=VMEM_LIMIT),
        name="rms_cast",
    )(x2, g.reshape(1, D_MODEL))


def _silu(z):
    hz = 0.5 * z
    return hz * jnp.tanh(hz) + hz


def _gelu_tanh(z):
    hz = 0.5 * z
    return hz * jnp.tanh(z * (GELU_K + (GELU_K * 0.044715) * (z * z))) + hz


def _sigmoid(z):
    return 0.5 * jnp.tanh(0.5 * z) + 0.5


def _in_proj_kernel(cols_ref, kinds_ref, scale_ref, h_ref, w_ref, z_ref, wb_ref):
    j = pl.program_id(0)

    @pl.when(pl.program_id(1) == 0)
    def _():
        wb_ref[...] = w_ref[0].astype(BF16)

    def project(act):
        for rows in (slice(r, r + SUB_M) for r in range(0, TM_IN, SUB_M)):
            z = jnp.dot(h_ref[rows, :], wb_ref[...], preferred_element_type=F32)
            z_ref[rows, :] = act(z).astype(BF16)

    acts = {KIND_LINEAR: lambda z: z * scale_ref[j], KIND_SILU: _silu,
            KIND_GELU: _gelu_tanh, KIND_SIGMOID: _sigmoid}
    for kind, act in acts.items():
        pl.when(kinds_ref[j] == kind)(functools.partial(project, act))


def _in_proj(h2, w_in, layer):
    n = h2.shape[0]
    grid_spec = pltpu.PrefetchScalarGridSpec(
        num_scalar_prefetch=2,
        grid=(len(Z_BLOCKS), n // TM_IN),
        in_specs=[
            pl.BlockSpec(memory_space=pltpu.SMEM),
            pl.BlockSpec((TM_IN, D_MODEL), lambda j, i, cols, kinds: (i, 0)),
            pl.BlockSpec((1, D_MODEL, D_BR), lambda j, i, cols, kinds: (layer, 0, cols[j])),
        ],
        out_specs=pl.BlockSpec((TM_IN, D_BR), lambda j, i, cols, kinds: (i, j)),
        scratch_shapes=[pltpu.VMEM((D_MODEL, D_BR), BF16)],
    )
    return pl.pallas_call(
        _in_proj_kernel,
        grid_spec=grid_spec,
        out_shape=jax.ShapeDtypeStruct((n, len(Z_BLOCKS) * D_BR), BF16),
        compiler_params=pltpu.CompilerParams(
            dimension_semantics=("arbitrary", "arbitrary"),
            vmem_limit_bytes=VMEM_LIMIT),
        name="in_proj",
    )(jnp.asarray(Z_BLOCKS, jnp.int32), jnp.asarray(Z_KINDS, jnp.int32),
      jnp.asarray(Z_SCALES, F32), h2, w_in)


def _in_proj_conv_kernel(h_ref, *refs):
    w_refs, (convw_ref, ob_ref, wb_ref) = refs[:len(CONV_BLOCKS)], refs[len(CONV_BLOCKS):]

    @pl.when(pl.program_id(1) == 0)
    def _():
        for k, w_ref in enumerate(w_refs):
            wb_ref[:, k * CONV_GROUP:(k + 1) * CONV_GROUP] = w_ref[0].astype(BF16)

    row = lax.broadcasted_iota(jnp.int32, (SUB_M, CONV_GROUP), 0)
    taps = [convw_ref[0, k:k + 1, :] for k in range(CONV_WIDTH)]
    before1 = before2 = jnp.zeros((1, CONV_GROUP), F32)
    for rows in (slice(r, r + SUB_M) for r in range(0, TM_IN, SUB_M)):
        z = jnp.dot(h_ref[rows, :], wb_ref[...], preferred_element_type=F32)
        c, hb, b, g = (z[:, k * CONV_GROUP:(k + 1) * CONV_GROUP] for k in range(len(CONV_BLOCKS)))
        ch = c * hb
        ch1 = jnp.where(row == 0, before1, pltpu.roll(ch, 1, 0))
        ch2 = jnp.where(row == 0, before2, jnp.where(row == 1, before1, pltpu.roll(ch, 2, 0)))
        conv = taps[0] * ch2 + taps[1] * ch1 + taps[2] * ch
        ob_ref[rows, :] = (b * conv * _silu(g)).astype(BF16)
        before1, before2 = ch[SUB_M - 1:SUB_M], ch[SUB_M - 2:SUB_M - 1]


def _in_proj_conv(h2, w_in, conv_w, layer, seq):
    assert seq == TM_IN, "a conv tile must be one whole sequence"
    n = h2.shape[0]
    groups_per_block = D_BR // CONV_GROUP
    w_spec = lambda col: pl.BlockSpec(
        (1, D_MODEL, CONV_GROUP), lambda j, i, col=col: (layer, 0, col * groups_per_block + j))
    return pl.pallas_call(
        _in_proj_conv_kernel,
        grid=(groups_per_block, n // TM_IN),
        in_specs=[pl.BlockSpec((TM_IN, D_MODEL), lambda j, i: (i, 0)),
                  *[w_spec(col) for col in CONV_BLOCKS],
                  pl.BlockSpec((1, CONV_WIDTH, CONV_GROUP), lambda j, i: (layer, 0, j))],
        out_specs=pl.BlockSpec((TM_IN, CONV_GROUP), lambda j, i: (i, j)),
        out_shape=jax.ShapeDtypeStruct((n, D_BR), BF16),
        scratch_shapes=[pltpu.VMEM((D_MODEL, D_BR), BF16)],
        compiler_params=pltpu.CompilerParams(
            dimension_semantics=("arbitrary", "arbitrary"),
            vmem_limit_bytes=VMEM_LIMIT),
        name="in_proj_conv",
    )(h2, *([w_in] * len(CONV_BLOCKS)), conv_w)


def _attn_kernel(q_ref, k_ref, v_ref, g_ref, rows_ref, o_ref, bias_ref):
    @pl.when(pl.program_id(0) == 0)
    def _():
        r = lax.broadcasted_iota(jnp.int32, (Q_TILE, BIAS_W), 0)
        band = lax.broadcasted_iota(jnp.int32, (Q_TILE, BIAS_W), 1)
        band_lo = jnp.where(r < CHUNK, 0, CHUNK)
        visible = (band >= band_lo) & (band < band_lo + (N_PREV + 1) * CHUNK)
        for h in range(N_HEADS):
            row = jnp.broadcast_to(rows_ref[h:h + 1, :], (Q_TILE, ROW_W))
            toe = pltpu.roll(row, ROW_W - Q_TILE, 1, stride=1, stride_axis=0)[:, :BIAS_W]
            bias_ref[h // 2, (h % 2) * Q_TILE:(h % 2 + 1) * Q_TILE, :] = jnp.where(visible, toe, NEG_INF)

    low_head = lax.broadcasted_iota(jnp.int32, (Q_TILE, LANES), 1) < HEAD_DIM
    col_slices = [slice(p * LANES, (p + 1) * LANES) for p in range(HEAD_PAIRS)]

    def tile(t, carry):
        queries = pl.ds(pl.multiple_of(t * Q_TILE, Q_TILE), Q_TILE)
        start = pl.multiple_of(jnp.maximum(t * Q_TILE - N_PREV * CHUNK, 0), Q_TILE)
        shift = pl.multiple_of(jnp.maximum(N_PREV * CHUNK - t * Q_TILE, 0), LANES)

        def scores(p):
            q = q_ref[0, queries, col_slices[p]]
            zero = jnp.zeros_like(q)
            q2 = jnp.concatenate([jnp.where(low_head, q, zero), jnp.where(low_head, zero, q)], axis=0)
            kw = k_ref[0, pl.ds(start, WIN), col_slices[p]]
            return lax.dot_general(q2, kw, (((1,), (1,)), ((), ())), preferred_element_type=F32)

        def softmax_parts(p, s):
            e_parts, denom_parts = [], []
            for rows in (slice(r, r + SOFTMAX_ROWS) for r in range(0, 2 * Q_TILE, SOFTMAX_ROWS)):
                sr = s[rows] + bias_ref[p, rows, pl.ds(shift, WIN)]
                er = jnp.exp2(sr - jnp.max(sr, axis=-1, keepdims=True))
                denom_parts.append(jnp.sum(er, axis=-1, keepdims=True))
                e_parts.append(er.astype(BF16))
            return jnp.concatenate(e_parts, axis=0), jnp.concatenate(denom_parts, axis=0)

        def weighted_values(p, e, denom):
            vw = v_ref[0, pl.ds(start, WIN), col_slices[p]]
            r = jnp.dot(e, vw, preferred_element_type=F32) / denom
            o = jnp.where(low_head, r[:Q_TILE], r[Q_TILE:])
            gate = g_ref[0, queries, col_slices[p]].astype(F32)
            o_ref[0, queries, col_slices[p]] = (o * gate).astype(BF16)

        s_next = scores(0)
        for p in range(HEAD_PAIRS):
            s = s_next
            if p + 1 < HEAD_PAIRS:
                s_next = scores(p + 1)
            weighted_values(p, *softmax_parts(p, s))
        return carry

    lax.fori_loop(0, q_ref.shape[1] // Q_TILE, tile, 0)


def _bias_rows(rel_bias):
    dist = N_PREV * CHUNK + Q_TILE - jnp.arange(ROW_W)
    idx = jnp.clip(dist, -(CHUNK - 1), MAX_REL) + (CHUNK - 1)
    return rel_bias[:, idx].astype(F32) * LOG2_E


def _attention(z3, bias_rows):
    b, s, _ = z3.shape
    return pl.pallas_call(
        _attn_kernel,
        grid=(b,),
        in_specs=[
            pl.BlockSpec((1, s, D_BR), lambda i: (i, 0, Z_Q)),
            pl.BlockSpec((1, s, D_BR), lambda i: (i, 0, Z_K)),
            pl.BlockSpec((1, s, D_BR), lambda i: (i, 0, Z_V)),
            pl.BlockSpec((1, s, D_BR), lambda i: (i, 0, Z_GA)),
            pl.BlockSpec((N_HEADS, ROW_W), lambda i: (0, 0)),
        ],
        out_specs=pl.BlockSpec((1, s, D_BR), lambda i: (i, 0, 0)),
        out_shape=jax.ShapeDtypeStruct((b, s, D_BR), BF16),
        scratch_shapes=[pltpu.VMEM((HEAD_PAIRS, 2 * Q_TILE, BIAS_W), F32)],
        compiler_params=pltpu.CompilerParams(
            dimension_semantics=("arbitrary",),
            vmem_limit_bytes=VMEM_LIMIT),
        name="attn",
    )(z3, z3, z3, z3, bias_rows)


def _mix_out_kernel(emit_h,
                    a0_ref, ga0_ref, a_ref, ga_ref, ob_ref, u_ref, vc_ref, sgc_ref, gb_ref, gc_ref,
                    x_ref, lng_ref, lnb_ref, spw_ref, spb_ref,
                    wa_ref, wb_ref, wc_ref, wout_ref, postg_ref, *rest):
    if emit_h:
        preg_ref, o_ref, h_ref, ya_ref = rest
    else:
        o_ref, ya_ref = rest
    tm = x_ref.shape[0]

    def gated_branch_a(act_ref, gate_ref):
        return gate_ref[...].astype(F32) * jnp.dot(act_ref[...], wa_ref[0], preferred_element_type=F32)

    @pl.when(pl.program_id(0) == 0)
    def _():
        ya_ref[...] = gated_branch_a(a0_ref, ga0_ref)

    merged = ya_ref[...]
    merged += gb_ref[...].astype(F32) * jnp.dot(ob_ref[...], wb_ref[0], preferred_element_type=F32)

    v = vc_ref[...].astype(F32)
    mu = jnp.mean(v, axis=-1, keepdims=True)
    vc = v - mu
    var = jnp.mean(vc * vc, axis=-1, keepdims=True)
    vn = (vc * lax.rsqrt(var + EPS) * lng_ref[...] + lnb_ref[...]).astype(BF16)
    tri = (lax.broadcasted_iota(jnp.int32, (SGU_CHUNK, SGU_CHUNK), 0)
           >= lax.broadcasted_iota(jnp.int32, (SGU_CHUNK, SGU_CHUNK), 1))
    oc_rows = []
    for n in range(tm // SGU_CHUNK):
        rows = slice(n * SGU_CHUNK, (n + 1) * SGU_CHUNK)
        oc_cols = []
        for g in range(N_GROUPS):
            cols = slice(g * GROUP_CH, (g + 1) * GROUP_CH)
            w = jnp.where(tri, spw_ref[0, g], 0.0).astype(BF16)
            mixed = jnp.dot(w, vn[rows, cols], preferred_element_type=F32) + spb_ref[g]
            oc_cols.append(mixed)
        oc_rows.append(jnp.concatenate(oc_cols, axis=1))
    mixed = jnp.concatenate(oc_rows, axis=0)
    oc = (u_ref[...].astype(F32) * mixed * sgc_ref[...].astype(F32)).astype(BF16)

    merged += gc_ref[...].astype(F32) * jnp.dot(oc, wc_ref[0], preferred_element_type=F32)
    y = jnp.dot(merged.astype(BF16), wout_ref[0], preferred_element_type=F32)
    ya_ref[...] = gated_branch_a(a_ref, ga_ref)
    r = lax.rsqrt(jnp.mean(y * y, axis=-1, keepdims=True) + EPS)
    xn = x_ref[...] + y * r * postg_ref[...]
    o_ref[...] = xn
    if emit_h:
        rn = lax.rsqrt(jnp.mean(xn * xn, axis=-1, keepdims=True) + EPS)
        h_ref[...] = (xn * rn * preg_ref[...]).astype(BF16)


def _mix_out(a2, ob2, z2, x2, layer, ln_g, ln_b, sp_w, sp_b,
             wa, wb, wc, wout, post_g, pre_g_next):
    n = x2.shape[0]
    tm = TM_OUT
    emit_h = pre_g_next is not None
    n_tiles = n // tm
    ahead = lambda i: jnp.minimum(i + 1, n_tiles - 1)
    zblk = lambda col: pl.BlockSpec((tm, D_BR), lambda i, col=col: (i, col))
    gate = lambda branch: pl.BlockSpec((tm, D_MODEL), lambda i, branch=branch: (i, branch))
    const = lambda shape: pl.BlockSpec(shape, lambda i: (0,) * len(shape),
                                       pipeline_mode=pl.Buffered(1))
    per_layer = lambda shape: pl.BlockSpec((1,) + shape, lambda i: (layer,) + (0,) * len(shape),
                                           pipeline_mode=pl.Buffered(1))
    row_tile = pl.BlockSpec((tm, D_MODEL), lambda i: (i, 0))
    in_specs = [
        pl.BlockSpec((tm, D_BR), lambda i: (0, 0), pipeline_mode=pl.Buffered(1)),
        pl.BlockSpec((tm, D_MODEL), lambda i: (0, 0), pipeline_mode=pl.Buffered(1)),
        pl.BlockSpec((tm, D_BR), lambda i: (ahead(i), 0)),
        pl.BlockSpec((tm, D_MODEL), lambda i: (ahead(i), 0)),
        zblk(0),
        zblk(Z_U), zblk(Z_VC), zblk(Z_GC),
        gate(1), gate(2),
        row_tile,
        const((1, D_BR)), const((1, D_BR)),
        per_layer((N_GROUPS, SGU_CHUNK, SGU_CHUNK)), const((N_GROUPS, SGU_CHUNK, 1)),
        per_layer((D_BR, D_MODEL)), per_layer((D_BR, D_MODEL)), per_layer((D_BR, D_MODEL)),
        per_layer((D_MODEL, D_MODEL)), const((1, D_MODEL)),
    ]
    args = [a2, z2, a2, z2, ob2, z2, z2, z2, z2, z2, x2,
            ln_g.reshape(1, D_BR), ln_b.reshape(1, D_BR),
            sp_w, sp_b.reshape(N_GROUPS, SGU_CHUNK, 1), wa, wb, wc, wout,
            post_g.reshape(1, D_MODEL)]
    out_specs = row_tile
    out_shape = jax.ShapeDtypeStruct((n, D_MODEL), F32)
    if emit_h:
        in_specs.append(const((1, D_MODEL)))
        args.append(pre_g_next.reshape(1, D_MODEL))
        out_specs = (row_tile, row_tile)
        out_shape = (out_shape, jax.ShapeDtypeStruct((n, D_MODEL), BF16))
    return pl.pallas_call(
        functools.partial(_mix_out_kernel, emit_h),
        grid=(n_tiles,),
        in_specs=in_specs,
        out_specs=out_specs,
        out_shape=out_shape,
        scratch_shapes=[pltpu.VMEM((tm, D_MODEL), F32)],
        compiler_params=pltpu.CompilerParams(
            dimension_semantics=("arbitrary",),
            vmem_limit_bytes=VMEM_LIMIT),
        name="mix_out",
    )(*args)


def kernel(x, pre_norm, w_in, rel_bias, conv_w, sgu_ln_g, sgu_ln_b, spatial_w, spatial_b,
           w_branch_a, w_branch_b, w_branch_c, w_out, post_norm):
    b, s, d = x.shape
    n = b * s
    depth = pre_norm.shape[0]
    x2 = x.reshape(n, d)
    wa, wb, wc, wout = (w.astype(BF16) for w in (w_branch_a, w_branch_b, w_branch_c, w_out))
    h2 = _rms_cast(x2, pre_norm[0])
    for i in range(depth):
        z2 = _in_proj(h2, w_in, i)
        ob = _in_proj_conv(h2, w_in, conv_w, i, s)
        a = _attention(z2.reshape(b, s, -1), _bias_rows(rel_bias[i]))
        res = _mix_out(a.reshape(n, D_BR), ob, z2, x2, i,
                       sgu_ln_g[i], sgu_ln_b[i], spatial_w, spatial_b[i],
                       wa, wb, wc, wout, post_norm[i],
                       pre_norm[i + 1] if i + 1 < depth else None)
        x2, h2 = res if i + 1 < depth else (res, None)
    return x2.reshape(b, s, d)
```
